```python
import jax, jax.numpy as jnp
from jax import lax
import numpy as np

D_MODEL = 2048
BATCH = 1
SEQ = 8192
DEPTH = 2
DEC_BATCH = 128
DEC_SEQ = 1
PAST_LEN = 2048
PAGE_SIZE = 128

N_HEADS = 8
HEAD_DIM = D_MODEL // 16
ATTN_W = N_HEADS * HEAD_DIM
CONV_CH = D_MODEL // 2
CONV_KERNEL = 31
N_GROUPS = 4
EXPERTS_PER_GROUP = 8
N_EXPERTS = N_GROUPS * EXPERTS_PER_GROUP
TOP_K_IN_GROUP = 2
D_EXPERT = D_MODEL // 8
Q_BLOCK = 128
EPS = 1e-6
SB_BIAS_INIT = -6.0
N_IN = 2 * CONV_CH + 3 * ATTN_W + 2 * D_MODEL

kernel_name = "hybrid_conformer_stickbreak_hmoe_step"


def rms_norm(x, g):
    xf = x.astype(jnp.float32)
    y = xf * lax.rsqrt(jnp.mean(xf * xf, axis=-1, keepdims=True) + EPS)
    return y.astype(x.dtype) * g


def layer_norm(x, g, b):
    xf = x.astype(jnp.float32)
    mu = jnp.mean(xf, axis=-1, keepdims=True)
    var = jnp.mean(jnp.square(xf - mu), axis=-1, keepdims=True)
    return ((xf - mu) * lax.rsqrt(var + EPS)).astype(x.dtype) * g + b


def stick_breaking(q, k, v, bias, q_pos, k_pos):
    z = jnp.einsum("bqhd,bkhd->bhqk", q, k).astype(jnp.float32) * (HEAD_DIM ** -0.5)
    z = z + bias.astype(jnp.float32)[None, :, None, None]
    mask = k_pos[None, :] < q_pos[:, None]
    log_1mb = jnp.where(mask, jax.nn.log_sigmoid(-z), 0.0)
    suffix = lax.cumsum(log_1mb, axis=3, reverse=True) - log_1mb
    a = jnp.where(mask, jnp.exp(jax.nn.log_sigmoid(z) + suffix), 0.0)
    return jnp.einsum("bhqk,bkhd->bqhd", a.astype(v.dtype), v)


def causal_dwconv(u, prev, w_dw, b_dw):
    buf = jnp.concatenate([prev.astype(u.dtype), u], axis=1)
    out = lax.conv_general_dilated(
        buf, w_dw[:, None, :].astype(u.dtype), window_strides=(1,), padding="VALID",
        dimension_numbers=("NWC", "WIO", "NWC"), feature_group_count=u.shape[-1])
    return out + b_dw, buf[:, -(CONV_KERNEL - 1):]


def hier_moe(h, w_rg, b_rg, w_re, b_re, w_eg, w_eu, w_ed):
    B, T, _ = h.shape
    pg = jax.nn.softmax((h @ w_rg + b_rg).astype(jnp.float32), axis=-1)
    p_grp, g_idx = lax.top_k(pg, 1)
    le = (h @ w_re + b_re).astype(jnp.float32).reshape(B, T, N_GROUPS, EXPERTS_PER_GROUP)
    le_sel = jnp.take_along_axis(le, g_idx[..., None], axis=2)[:, :, 0]
    pe = jax.nn.softmax(le_sel, axis=-1)
    p_top, e_idx = lax.top_k(pe, TOP_K_IN_GROUP)
    wts = p_grp * p_top / jnp.sum(p_top, axis=-1, keepdims=True)
    glob = g_idx * EXPERTS_PER_GROUP + e_idx
    gate = jnp.sum(jax.nn.one_hot(glob, N_EXPERTS, dtype=jnp.float32) * wts[..., None], axis=2)
    hg = jnp.einsum("btd,edf->btef", h, w_eg)
    hu = jnp.einsum("btd,edf->btef", h, w_eu)
    act = jax.nn.silu(hg) * hu * gate[..., None].astype(h.dtype)
    return jnp.einsum("btef,efd->btd", act, w_ed)


def layer(x, c, prev_conv, attend, w_ada, b_ada, g_n1, w_in, b_sb, w_dw, b_dw, ln_g, ln_b,
          w_conv_out, w_attn_out, w_o, g_n2, w_rg, b_rg, w_re, b_re, w_eg, w_eu, w_ed):
    B, T, _ = x.shape
    mod = (jax.nn.silu(c) @ w_ada + b_ada)[:, None, :]
    sh1, sc1, gt1, sh2, sc2, gt2 = jnp.split(mod, 6, axis=-1)
    h = rms_norm(x, g_n1) * (1 + sc1) + sh1
    proj = h @ w_in
    c2, a1 = 2 * CONV_CH, 2 * CONV_CH + ATTN_W
    ua, ub, q, k, v, ga, gb = jnp.split(
        proj, [CONV_CH, c2, a1, a1 + ATTN_W, a1 + 2 * ATTN_W, a1 + 2 * ATTN_W + D_MODEL], axis=-1)
    u = ua * jax.nn.sigmoid(ub)
    dc, new_conv = causal_dwconv(u, prev_conv, w_dw, b_dw)
    conv_out = jax.nn.silu(layer_norm(dc, ln_g, ln_b)) @ w_conv_out
    q = q.reshape(B, T, N_HEADS, HEAD_DIM)
    k = k.reshape(B, T, N_HEADS, HEAD_DIM)
    v = v.reshape(B, T, N_HEADS, HEAD_DIM)
    att = attend(q, k, v, b_sb).reshape(B, T, ATTN_W) @ w_attn_out
    merged = jax.nn.sigmoid(ga) * conv_out + jax.nn.sigmoid(gb) * att
    x = x + gt1 * (merged @ w_o)
    h2 = rms_norm(x, g_n2) * (1 + sc2) + sh2
    x = x + gt2 * hier_moe(h2, w_rg, b_rg, w_re, b_re, w_eg, w_eu, w_ed)
    return x, k, v, new_conv


def prompt_attend(q, k, v, bias):
    B, T = q.shape[0], q.shape[1]
    nb = T // Q_BLOCK
    qb = q.reshape(B, nb, Q_BLOCK, N_HEADS, HEAD_DIM).swapaxes(0, 1)
    pb = jnp.arange(T, dtype=jnp.int32).reshape(nb, Q_BLOCK)
    kpos = jnp.arange(T, dtype=jnp.int32)
    ob = lax.map(lambda a: stick_breaking(a[0], k, v, bias, a[1], kpos), (qb, pb))
    return ob.swapaxes(0, 1).reshape(B, T, N_HEADS, HEAD_DIM)


def setup_inputs(seed: int = 0) -> dict:
    key = jax.random.key(seed)
    ks = iter(jax.random.split(key, 40))
    f32 = jnp.float32
    n_pages = PAST_LEN // PAGE_SIZE
    used = DEC_BATCH * n_pages
    n_pool = used + max(1, used // 4)

    def nrm(shape, scale):
        return jax.random.normal(next(ks), shape, f32) * scale

    def gain(shape):
        return 1.0 + nrm(shape, 0.01)

    L = DEPTH
    page_table = jax.random.permutation(next(ks), n_pool)[:used].reshape(DEC_BATCH, n_pages).astype(jnp.int32)
    return {
        "x_prompt": nrm((BATCH, SEQ, D_MODEL), 1.0),
        "x_sample": nrm((DEC_BATCH, DEC_SEQ, D_MODEL), 1.0),
        "cache_k": nrm((L, n_pool, PAGE_SIZE, N_HEADS, HEAD_DIM), 1.0),
        "cache_v": nrm((L, n_pool, PAGE_SIZE, N_HEADS, HEAD_DIM), 1.0),
        "state_conv": nrm((L, DEC_BATCH, CONV_KERNEL - 1, CONV_CH), 1.0),
        "page_table": page_table,
        "c_prompt": nrm((BATCH, D_MODEL), 1.0),
        "c_sample": nrm((DEC_BATCH, D_MODEL), 1.0),
        "w_ada": nrm((L, D_MODEL, 6 * D_MODEL), 0.5 * D_MODEL ** -0.5),
        "b_ada": nrm((L, 6 * D_MODEL), 0.01),
        "g_n1": gain((L, D_MODEL)),
        "w_in": nrm((L, D_MODEL, N_IN), D_MODEL ** -0.5),
        "b_sb": SB_BIAS_INIT + nrm((L, N_HEADS), 0.1),
        "w_dw": nrm((L, CONV_KERNEL, CONV_CH), CONV_KERNEL ** -0.5),
        "b_dw": nrm((L, CONV_CH), 0.01),
        "ln_g": gain((L, CONV_CH)),
        "ln_b": nrm((L, CONV_CH), 0.01),
        "w_conv_out": nrm((L, CONV_CH, D_MODEL), CONV_CH ** -0.5),
        "w_attn_out": nrm((L, ATTN_W, D_MODEL), ATTN_W ** -0.5),
        "w_o": nrm((L, D_MODEL, D_MODEL), D_MODEL ** -0.5),
        "g_n2": gain((L, D_MODEL)),
        "w_rg": nrm((L, D_MODEL, N_GROUPS), D_MODEL ** -0.5),
        "b_rg": nrm((L, N_GROUPS), 0.01),
        "w_re": nrm((L, D_MODEL, N_EXPERTS), D_MODEL ** -0.5),
        "b_re": nrm((L, N_EXPERTS), 0.01),
        "w_eg": nrm((L, N_EXPERTS, D_MODEL, D_EXPERT), D_MODEL ** -0.5),
        "w_eu": nrm((L, N_EXPERTS, D_MODEL, D_EXPERT), D_MODEL ** -0.5),
        "w_ed": nrm((L, N_EXPERTS, D_EXPERT, D_MODEL), D_EXPERT ** -0.5),
        "g_final": gain((D_MODEL,)),
    }


def reference(x_prompt, x_sample, cache_k, cache_v, state_conv, page_table, c_prompt, c_sample,
              w_ada, b_ada, g_n1, w_in, b_sb, w_dw, b_dw, ln_g, ln_b, w_conv_out, w_attn_out, w_o,
              g_n2, w_rg, b_rg, w_re, b_re, w_eg, w_eu, w_ed, g_final):
    n_seq = page_table.shape[0]
    past = page_table.shape[1] * PAGE_SIZE
    t_new = x_sample.shape[1]
    xp, xs = x_prompt, x_sample
    kp_l, vp_l, cp_l, ks_l, vs_l, cs_l = [], [], [], [], [], []
    for l in range(DEPTH):
        lp = (w_ada[l], b_ada[l], g_n1[l], w_in[l], b_sb[l], w_dw[l], b_dw[l], ln_g[l], ln_b[l],
              w_conv_out[l], w_attn_out[l], w_o[l], g_n2[l], w_rg[l], b_rg[l], w_re[l], b_re[l],
              w_eg[l], w_eu[l], w_ed[l])
        zero_conv = jnp.zeros((xp.shape[0], CONV_KERNEL - 1, CONV_CH), xp.dtype)
        xp, kp, vp, cp = layer(xp, c_prompt, zero_conv, prompt_attend, *lp)
        k_past = cache_k[l][page_table].reshape(n_seq, past, N_HEADS, HEAD_DIM)
        v_past = cache_v[l][page_table].reshape(n_seq, past, N_HEADS, HEAD_DIM)

        def sample_attend(q, k, v, bias, k_past=k_past, v_past=v_past):
            keys = jnp.concatenate([k_past.astype(k.dtype), k], axis=1)
            vals = jnp.concatenate([v_past.astype(v.dtype), v], axis=1)
            kpos = jnp.arange(past + t_new, dtype=jnp.int32)
            qpos = past + jnp.arange(t_new, dtype=jnp.int32)
            return stick_breaking(q, keys, vals, bias, qpos, kpos)

        xs, ksn, vsn, csn = layer(xs, c_sample, state_conv[l], sample_attend, *lp)
        kp_l.append(kp); vp_l.append(vp); cp_l.append(cp)
        ks_l.append(ksn); vs_l.append(vsn); cs_l.append(csn)
    y_prompt = rms_norm(xp, g_final)
    y_sample = rms_norm(xs, g_final)
    return (y_prompt, y_sample, jnp.stack(kp_l), jnp.stack(vp_l), jnp.stack(cp_l),
            jnp.stack(ks_l), jnp.stack(vs_l), jnp.stack(cs_l))
```

```python
import functools

import jax
import jax.numpy as jnp
from jax import lax
from jax.experimental import pallas as pl
from jax.experimental.pallas import tpu as pltpu

F32 = jnp.float32
BF16 = jnp.bfloat16

N_HEADS = 8
HEAD_DIM = 128
CONV_KERNEL = 31
N_GROUPS = 4
EXPERTS_PER_GROUP = 8
N_EXPERTS = N_GROUPS * EXPERTS_PER_GROUP
EPS = 1e-6

LANES = 128
V7X_VMEM_BYTES = 64 * 1024 * 1024
VMEM_LIMIT = V7X_VMEM_BYTES - 8 * 1024 * 1024


def _cparams(*sem):
    return pltpu.CompilerParams(dimension_semantics=sem, vmem_limit_bytes=VMEM_LIMIT)


def _dot(a, b):
    return jnp.dot(a, b, preferred_element_type=F32)


def _dot_nt(a, b):
    return lax.dot_general(a, b, (((1,), (1,)), ((), ())), preferred_element_type=F32)


def _silu(x):
    return x * jax.nn.sigmoid(x)


def _row_spec(rows, tm, width):
    if rows == 1:
        return pl.BlockSpec((1, width), lambda i, *_: (0, 0))
    return pl.BlockSpec((tm, width), lambda i, *_: (i, 0))


def _ada_kernel(c_ref, w_ref, b_ref, o_ref):
    a = _silu(c_ref[...]).astype(BF16)
    o_ref[0] = _dot(a, w_ref[0].astype(BF16)) + b_ref[0]


def ada_mod(c_all, w_ada, b_ada, tn=512):
    n_layers, d, n = w_ada.shape
    rows = c_all.shape[0]
    return pl.pallas_call(
        _ada_kernel,
        grid=(n_layers, n // tn),
        in_specs=[
            pl.BlockSpec((rows, d), lambda l, j: (0, 0)),
            pl.BlockSpec((1, d, tn), lambda l, j: (l, 0, j)),
            pl.BlockSpec((1, 1, tn), lambda l, j: (l, 0, j)),
        ],
        out_specs=pl.BlockSpec((1, rows, tn), lambda l, j: (l, 0, j)),
        out_shape=jax.ShapeDtypeStruct((n_layers, rows, n), F32),
        compiler_params=_cparams("parallel", "parallel"),
        name="ada_mod",
    )(c_all, w_ada, b_ada.reshape(n_layers, 1, n))


def _norm_mod_kernel(x_ref, g_ref, sc_ref, sh_ref, o_ref):
    x = x_ref[...]
    y = x * lax.rsqrt(jnp.mean(x * x, axis=-1, keepdims=True) + EPS)
    o_ref[...] = ((y * g_ref[...]) * (1.0 + sc_ref[...]) + sh_ref[...]).astype(o_ref.dtype)


def norm_mod(x, g, sc, sh, out_dtype, tm):
    t, d = x.shape
    return pl.pallas_call(
        _norm_mod_kernel,
        grid=(t // tm,),
        in_specs=[
            pl.BlockSpec((tm, d), lambda i: (i, 0)),
            pl.BlockSpec((1, d), lambda i: (0, 0)),
            _row_spec(sc.shape[0], tm, d),
            _row_spec(sh.shape[0], tm, d),
        ],
        out_specs=pl.BlockSpec((tm, d), lambda i: (i, 0)),
        out_shape=jax.ShapeDtypeStruct((t, d), out_dtype),
        compiler_params=_cparams("parallel"),
        name="norm_mod",
    )(x, g.reshape(1, d), sc, sh)


def _glu_kernel(h_ref, wa_ref, wb_ref, u_ref):
    h = h_ref[...]
    u_ref[...] = _dot(h, wa_ref[...]) * jax.nn.sigmoid(_dot(h, wb_ref[...]))


def glu_proj(h, wa, wb, tm, tn=512):
    t, d = h.shape
    n = wa.shape[1]
    return pl.pallas_call(
        _glu_kernel,
        grid=(t // tm, n // tn),
        in_specs=[
            pl.BlockSpec((tm, d), lambda i, j: (i, 0)),
            pl.BlockSpec((d, tn), lambda i, j: (0, j)),
            pl.BlockSpec((d, tn), lambda i, j: (0, j)),
        ],
        out_specs=pl.BlockSpec((tm, tn), lambda i, j: (i, j)),
        out_shape=jax.ShapeDtypeStruct((t, n), F32),
        compiler_params=_cparams("parallel", "parallel"),
        name="glu_proj",
    )(h, wa, wb)


def _proj_kernel(h_ref, w_ref, *o_refs):
    r = _dot(h_ref[...], w_ref[...])
    for o_ref in o_refs:
        o_ref[...] = r.astype(o_ref.dtype)


def proj(h, w, out_dtypes, tm, tn=512):
    t, d = h.shape
    n = w.shape[1]
    return pl.pallas_call(
        _proj_kernel,
        grid=(t // tm, n // tn),
        in_specs=[
            pl.BlockSpec((tm, d), lambda i, j: (i, 0)),
            pl.BlockSpec((d, tn), lambda i, j: (0, j)),
        ],
        out_specs=[pl.BlockSpec((tm, tn), lambda i, j: (i, j)) for _ in out_dtypes],
        out_shape=[jax.ShapeDtypeStruct((t, n), dt) for dt in out_dtypes],
        compiler_params=_cparams("parallel", "parallel"),
        name="proj",
    )(h, w)


def _layer_norm_silu(dc, g, b):
    mu = jnp.mean(dc, axis=-1, keepdims=True)
    xc = dc - mu
    var = jnp.mean(xc * xc, axis=-1, keepdims=True)
    return _silu((xc * lax.rsqrt(var + EPS)) * g + b)


HALO = 32


def _conv_prompt_kernel(u_ref, halo_ref, prev_ref, w_ref, b_ref, g_ref, beta_ref, o_ref, buf_ref, dc_ref):
    i = pl.program_id(0)
    tt, c = u_ref.shape

    @pl.when(i == 0)
    def _():
        buf_ref[0:HALO, :] = prev_ref[...]

    @pl.when(i > 0)
    def _():
        buf_ref[0:HALO, :] = halo_ref[...]

    buf_ref[HALO:HALO + tt, :] = u_ref[...]
    first = HALO - (CONV_KERNEL - 1)
    for cc in range(c // LANES):
        ls = slice(cc * LANES, (cc + 1) * LANES)
        acc = jnp.broadcast_to(b_ref[:, ls], (tt, LANES))
        for j in range(CONV_KERNEL):
            acc = acc + w_ref[j:j + 1, ls] * buf_ref[first + j:first + j + tt, ls]
        dc_ref[:, ls] = acc
    o_ref[...] = _layer_norm_silu(dc_ref[...], g_ref[...], beta_ref[...]).astype(o_ref.dtype)


def conv_prompt(u, prev_halo, w_dw, b_dw, ln_g, ln_b, tt=256):
    t, c = u.shape
    ratio = tt // HALO
    return pl.pallas_call(
        _conv_prompt_kernel,
        grid=(t // tt,),
        in_specs=[
            pl.BlockSpec((tt, c), lambda i: (i, 0)),
            pl.BlockSpec((HALO, c), lambda i: (jnp.maximum(i * ratio - 1, 0), 0)),
            pl.BlockSpec((HALO, c), lambda i: (0, 0)),
            pl.BlockSpec((CONV_KERNEL, c), lambda i: (0, 0)),
            pl.BlockSpec((1, c), lambda i: (0, 0)),
            pl.BlockSpec((1, c), lambda i: (0, 0)),
            pl.BlockSpec((1, c), lambda i: (0, 0)),
        ],
        out_specs=pl.BlockSpec((tt, c), lambda i: (i, 0)),
        out_shape=jax.ShapeDtypeStruct((t, c), BF16),
        scratch_shapes=[pltpu.VMEM((HALO + tt, c), F32), pltpu.VMEM((tt, c), F32)],
        compiler_params=_cparams("arbitrary"),
        name="conv_prompt",
    )(u, u, prev_halo, w_dw, b_dw.reshape(1, c), ln_g.reshape(1, c), ln_b.reshape(1, c))


def _conv_sample_kernel(u_ref, prev_ref, w_ref, b_ref, g_ref, beta_ref, act_ref, new_ref, dc_ref):
    bb = u_ref.shape[0]
    hist = CONV_KERNEL - 1
    w_hist = w_ref[0:hist, :]
    w_last = w_ref[hist:hist + 1, :]
    for b in range(bb):
        prev = prev_ref[b]
        u_row = u_ref[b:b + 1, :]
        dc_ref[b:b + 1, :] = (jnp.sum(prev * w_hist, axis=0, keepdims=True) + w_last * u_row + b_ref[...])
        new_ref[b, 0:hist - 1, :] = prev_ref[b, 1:hist, :]
        new_ref[b, hist - 1:hist, :] = u_row
    act_ref[...] = _layer_norm_silu(dc_ref[...], g_ref[...], beta_ref[...]).astype(act_ref.dtype)


def conv_sample(u, prev, w_dw, b_dw, ln_g, ln_b, bb=8):
    n, c = u.shape
    hist = prev.shape[1]
    return pl.pallas_call(
        _conv_sample_kernel,
        grid=(n // bb,),
        in_specs=[
            pl.BlockSpec((bb, c), lambda i: (i, 0)),
            pl.BlockSpec((bb, hist, c), lambda i: (i, 0, 0)),
            pl.BlockSpec((CONV_KERNEL, c), lambda i: (0, 0)),
            pl.BlockSpec((1, c), lambda i: (0, 0)),
            pl.BlockSpec((1, c), lambda i: (0, 0)),
            pl.BlockSpec((1, c), lambda i: (0, 0)),
        ],
        out_specs=[
            pl.BlockSpec((bb, c), lambda i: (i, 0)),
            pl.BlockSpec((bb, hist, c), lambda i: (i, 0, 0)),
        ],
        out_shape=[jax.ShapeDtypeStruct((n, c), BF16), jax.ShapeDtypeStruct((n, hist, c), F32)],
        scratch_shapes=[pltpu.VMEM((bb, c), F32)],
        compiler_params=_cparams("parallel"),
        name="conv_sample",
    )(u, prev, w_dw, b_dw.reshape(1, c), ln_g.reshape(1, c), ln_b.reshape(1, c))


def _softplus(z):
    return jnp.maximum(z, 0.0) + jnp.log(1.0 + jnp.exp(-jnp.abs(z)))


def _strict_lower(n):
    return (lax.broadcasted_iota(jnp.int32, (n, n), 0) > lax.broadcasted_iota(jnp.int32, (n, n), 1)).astype(BF16)


def _attn_prompt_kernel(bias_ref, q_ref, k_ref, v_ref, o_ref, acc_ref, carry_ref, *, tk, scale):
    qi = pl.program_id(1)
    tq = q_ref.shape[0]
    n_diag = tq // tk
    q = q_ref[...]
    bias = bias_ref[0]
    upper = _strict_lower(tk)
    acc_ref[...] = jnp.zeros_like(acc_ref)
    carry_ref[...] = jnp.zeros_like(carry_ref)

    def tile(kb, masked):
        start = pl.multiple_of(kb * tk, tk)
        ks = k_ref[pl.ds(start, tk), :]
        vs = v_ref[pl.ds(start, tk), :]
        z = _dot_nt(q, ks) * scale + bias
        sp = _softplus(z)
        ls = z - sp
        if masked:
            t_pos = qi * tq + lax.broadcasted_iota(jnp.int32, (tq, tk), 0)
            s_pos = kb * tk + lax.broadcasted_iota(jnp.int32, (tq, tk), 1)
            valid = s_pos < t_pos
            sp = jnp.where(valid, sp, 0.0)
        spb = sp.astype(BF16)
        later = _dot(spb, upper)
        carry = carry_ref[...]
        a = jnp.exp(ls - later - carry)
        if masked:
            a = jnp.where(valid, a, 0.0)
        acc_ref[...] += _dot(a.astype(BF16), vs)
        carry_ref[...] = carry + later[:, 0:1] + spb[:, 0:1].astype(F32)

    for d in range(n_diag):
        tile(qi * n_diag + (n_diag - 1 - d), True)

    def body(j, c):
        tile(qi * n_diag - 1 - j, False)
        return c

    lax.fori_loop(0, qi * n_diag, body, 0)
    o_ref[...] = acc_ref[...].astype(o_ref.dtype)


def attn_prompt(q, k, v, b_sb, tq=512, tk=256):
    t = q.shape[0]
    tq = min(tq, t)
    bias = jnp.broadcast_to(b_sb.astype(F32)[:, None, None], (N_HEADS, 1, tk))
    kern = functools.partial(_attn_prompt_kernel, tk=tk, scale=HEAD_DIM ** -0.5)
    return pl.pallas_call(
        kern,
        grid=(N_HEADS, t // tq),
        in_specs=[
            pl.BlockSpec((1, 1, tk), lambda h, i: (h, 0, 0)),
            pl.BlockSpec((tq, HEAD_DIM), lambda h, i: (i, h)),
            pl.BlockSpec((t, HEAD_DIM), lambda h, i: (0, h)),
            pl.BlockSpec((t, HEAD_DIM), lambda h, i: (0, h)),
        ],
        out_specs=pl.BlockSpec((tq, HEAD_DIM), lambda h, i: (i, h)),
        out_shape=jax.ShapeDtypeStruct(q.shape, BF16),
        scratch_shapes=[pltpu.VMEM((tq, HEAD_DIM), F32), pltpu.VMEM((tq, 1), F32)],
        compiler_params=_cparams("parallel", "arbitrary"),
        name="attn_prompt",
    )(bias, q, k, v)


def _attn_sample_kernel(pt_ref, bias_ref, q_ref, *refs, pp, page, scale):
    k_refs, v_refs = refs[:pp], refs[pp:2 * pp]
    o_ref, acc_ref, carry_ref = refs[2 * pp:]
    j = pl.program_id(1)

    @pl.when(j == 0)
    def _():
        acc_ref[...] = jnp.zeros_like(acc_ref)
        carry_ref[...] = jnp.zeros_like(carry_ref)

    q = q_ref[0]
    head = lax.broadcasted_iota(jnp.int32, (N_HEADS, LANES), 0)
    upper = _strict_lower(page)
    bias = bias_ref[...]
    acc = acc_ref[...]
    carry = carry_ref[...]
    for p in reversed(range(pp)):
        z = jnp.zeros((N_HEADS, page), F32)
        for h in range(N_HEADS):
            kh = k_refs[p][0, 0, pl.ds(h, page, stride=N_HEADS), :]
            z = jnp.where(head == h, _dot_nt(q, kh.astype(BF16)), z)
        z = z * scale + bias
        sp = _softplus(z)
        spb = sp.astype(BF16)
        later = _dot(spb, upper)
        a = jnp.exp((z - sp) - later - carry).astype(BF16)
        for h in range(N_HEADS):
            vh = v_refs[p][0, 0, pl.ds(h, page, stride=N_HEADS), :]
            acc = jnp.where(head == h, acc + _dot(a, vh.astype(BF16)), acc)
        carry = carry + later[:, 0:1] + spb[:, 0:1].astype(F32)
    acc_ref[...] = acc
    carry_ref[...] = carry

    @pl.when(j == pl.num_programs(1) - 1)
    def _():
        o_ref[0] = acc.astype(o_ref.dtype)


def attn_sample(q, cache_k, cache_v, layer, page_table, b_sb, pp=8):
    n_seq, n_pages = page_table.shape
    n_layers, n_pool, page = cache_k.shape[:3]
    pp = min(pp, n_pages)
    steps = n_pages // pp
    ck = cache_k.reshape(n_layers, n_pool, page * N_HEADS, HEAD_DIM)
    cv = cache_v.reshape(n_layers, n_pool, page * N_HEADS, HEAD_DIM)
    bias = jnp.broadcast_to(b_sb.astype(F32)[:, None], (N_HEADS, page))

    def page_spec(p):
        def index(b, j, pt):
            return (layer, pt[b * n_pages + (steps - 1 - j) * pp + p], 0, 0)
        return pl.BlockSpec((1, 1, page * N_HEADS, HEAD_DIM), index)

    kern = functools.partial(_attn_sample_kernel, pp=pp, page=page, scale=HEAD_DIM ** -0.5)
    grid_spec = pltpu.PrefetchScalarGridSpec(
        num_scalar_prefetch=1,
        grid=(n_seq, steps),
        in_specs=[
            pl.BlockSpec((N_HEADS, page), lambda b, j, pt: (0, 0)),
            pl.BlockSpec((1, N_HEADS, HEAD_DIM), lambda b, j, pt: (b, 0, 0)),
            *[page_spec(p) for p in range(pp)],
            *[page_spec(p) for p in range(pp)],
        ],
        out_specs=pl.BlockSpec((1, N_HEADS, HEAD_DIM), lambda b, j, pt: (b, 0, 0)),
        scratch_shapes=[pltpu.VMEM((N_HEADS, HEAD_DIM), F32), pltpu.VMEM((N_HEADS, 1), F32)],
    )
    return pl.pallas_call(
        kern,
        grid_spec=grid_spec,
        out_shape=jax.ShapeDtypeStruct((n_seq, N_HEADS, HEAD_DIM), BF16),
        compiler_params=_cparams("parallel", "arbitrary"),
        name="attn_sample",
    )(page_table.reshape(-1), bias, q, *([ck] * pp), *([cv] * pp))


def _merge_kernel(h_ref, act_ref, att_ref, wga_ref, wgb_ref, wco_ref, wao_ref, o_ref):
    h = h_ref[...]
    conv_out = _dot(act_ref[...], wco_ref[...])
    att_out = _dot(att_ref[...], wao_ref[...])
    ga = jax.nn.sigmoid(_dot(h, wga_ref[...]))
    gb = jax.nn.sigmoid(_dot(h, wgb_ref[...]))
    o_ref[...] = (ga * conv_out + gb * att_out).astype(o_ref.dtype)


def merge(h, act, att, wga, wgb, wco, wao, tm, tn=512):
    t, d = h.shape
    c = act.shape[1]
    n = wga.shape[1]
    return pl.pallas_call(
        _merge_kernel,
        grid=(t // tm, n // tn),
        in_specs=[
            pl.BlockSpec((tm, d), lambda i, j: (i, 0)),
            pl.BlockSpec((tm, c), lambda i, j: (i, 0)),
            pl.BlockSpec((tm, c), lambda i, j: (i, 0)),
            pl.BlockSpec((d, tn), lambda i, j: (0, j)),
            pl.BlockSpec((d, tn), lambda i, j: (0, j)),
            pl.BlockSpec((c, tn), lambda i, j: (0, j)),
            pl.BlockSpec((c, tn), lambda i, j: (0, j)),
        ],
        out_specs=pl.BlockSpec((tm, tn), lambda i, j: (i, j)),
        out_shape=jax.ShapeDtypeStruct((t, n), BF16),
        compiler_params=_cparams("parallel", "parallel"),
        name="merge",
    )(h, act, att, wga, wgb, wco, wao)


def _first_argmax(x, m, lane):
    return jnp.min(jnp.where(x == m, lane, float(LANES)), axis=-1, keepdims=True)


def _router_gate(lg, le):
    lane = lax.broadcasted_iota(jnp.int32, lg.shape, 1).astype(F32)
    neg = -jnp.inf
    lg = jnp.where(lane < N_GROUPS, lg, neg)
    mg = jnp.max(lg, axis=-1, keepdims=True)
    p_grp = 1.0 / jnp.sum(jnp.exp(lg - mg), axis=-1, keepdims=True)
    g_idx = _first_argmax(lg, mg, lane)
    in_group = jnp.floor(lane * (1.0 / EXPERTS_PER_GROUP)) == g_idx
    le = jnp.where(in_group, le, neg)
    m1 = jnp.max(le, axis=-1, keepdims=True)
    i1 = _first_argmax(le, m1, lane)
    le2 = jnp.where(lane == i1, neg, le)
    m2 = jnp.max(le2, axis=-1, keepdims=True)
    i2 = _first_argmax(le2, m2, lane)
    e2 = jnp.exp(m2 - m1)
    w1 = p_grp / (1.0 + e2)
    return jnp.where(lane == i1, w1, 0.0) + jnp.where(lane == i2, w1 * e2, 0.0)


def _split_dot(x_hi, x_lo, w_hi_ref, w_lo_ref):
    w_hi = w_hi_ref[...]
    return _dot(x_hi, w_hi) + (_dot(x_lo, w_hi) + _dot(x_hi, w_lo_ref[...]))


def _oproj_kernel(m_ref, wo_ref, x_ref, gt_ref, g2_ref, sc_ref, sh_ref, wgh_ref, wgl_ref, weh_ref, wel_ref,
                  bg_ref, be_ref, x1_ref, h2_ref, gate_ref):
    x1 = x_ref[...] + gt_ref[...] * _dot(m_ref[...], wo_ref[...])
    x1_ref[...] = x1
    y = x1 * lax.rsqrt(jnp.mean(x1 * x1, axis=-1, keepdims=True) + EPS)
    h2 = (y * g2_ref[...]) * (1.0 + sc_ref[...]) + sh_ref[...]
    hi = h2.astype(BF16)
    h2_ref[...] = hi
    lo = (h2 - hi.astype(F32)).astype(BF16)
    lg = _split_dot(hi, lo, wgh_ref, wgl_ref) + bg_ref[...]
    le = _split_dot(hi, lo, weh_ref, wel_ref) + be_ref[...]
    gate_ref[...] = _router_gate(lg, le)


def _pad_lanes(w):
    return jnp.pad(w, ((0, 0), (0, LANES - w.shape[-1])))


def _hi_lo(w):
    hi = w.astype(BF16)
    return hi, (w - hi.astype(F32)).astype(BF16)


def oproj_router(m, wo, x, gt, g2, sc, sh, w_rg, b_rg, w_re, b_re, tm):
    t, d = x.shape
    wgh, wgl = _hi_lo(_pad_lanes(w_rg))
    weh, wel = _hi_lo(_pad_lanes(w_re))
    bg = _pad_lanes(b_rg.reshape(1, -1))
    be = _pad_lanes(b_re.reshape(1, -1))
    full = lambda shape: pl.BlockSpec(shape, lambda i: (0, 0))
    return pl.pallas_call(
        _oproj_kernel,
        grid=(t // tm,),
        in_specs=[
            pl.BlockSpec((tm, d), lambda i: (i, 0)),
            full((d, d)),
            pl.BlockSpec((tm, d), lambda i: (i, 0)),
            _row_spec(gt.shape[0], tm, d),
            full((1, d)),
            _row_spec(sc.shape[0], tm, d),
            _row_spec(sh.shape[0], tm, d),
            full((d, LANES)), full((d, LANES)), full((d, LANES)), full((d, LANES)),
            full((1, LANES)), full((1, LANES)),
        ],
        out_specs=[
            pl.BlockSpec((tm, d), lambda i: (i, 0)),
            pl.BlockSpec((tm, d), lambda i: (i, 0)),
            pl.BlockSpec((tm, LANES), lambda i: (i, 0)),
        ],
        out_shape=[
            jax.ShapeDtypeStruct((t, d), F32),
            jax.ShapeDtypeStruct((t, d), BF16),
            jax.ShapeDtypeStruct((t, LANES), F32),
        ],
        compiler_params=_cparams("parallel"),
        name="oproj_router",
    )(m, wo, x, gt, g2.reshape(1, d), sc, sh, wgh, wgl, weh, wel, bg, be)


def _moe_kernel(h_ref, gate_ref, wg_ref, wu_ref, wd_ref, x_ref, gt_ref, o_ref, acc_ref):
    e = pl.program_id(1)

    @pl.when(e == 0)
    def _():
        acc_ref[...] = jnp.zeros_like(acc_ref)

    h = h_ref[...]
    gate = gate_ref[...]
    lane = lax.broadcasted_iota(jnp.int32, gate.shape, 1)
    ge = jnp.sum(jnp.where(lane == e, gate, 0.0), axis=-1, keepdims=True)
    act = (_silu(_dot(h, wg_ref[0])) * _dot(h, wu_ref[0])) * ge
    acc_ref[...] += _dot(act.astype(BF16), wd_ref[0])

    @pl.when(e == pl.num_programs(1) - 1)
    def _():
        o_ref[...] = x_ref[...] + gt_ref[...] * acc_ref[...]


def moe_dense(h2, gate, w_eg, w_eu, w_ed, x1, gt, tm):
    t, d = x1.shape
    n_e, _, f = w_eg.shape
    return pl.pallas_call(
        _moe_kernel,
        grid=(t // tm, n_e),
        in_specs=[
            pl.BlockSpec((tm, d), lambda i, e: (i, 0)),
            pl.BlockSpec((tm, LANES), lambda i, e: (i, 0)),
            pl.BlockSpec((1, d, f), lambda i, e: (e, 0, 0)),
            pl.BlockSpec((1, d, f), lambda i, e: (e, 0, 0)),
            pl.BlockSpec((1, f, d), lambda i, e: (e, 0, 0)),
            pl.BlockSpec((tm, d), lambda i, e: (i, 0)),
            _row_spec(gt.shape[0], tm, d),
        ],
        out_specs=pl.BlockSpec((tm, d), lambda i, e: (i, 0)),
        out_shape=jax.ShapeDtypeStruct((t, d), F32),
        scratch_shapes=[pltpu.VMEM((tm, d), F32)],
        compiler_params=_cparams("parallel", "arbitrary"),
        name="moe_dense",
    )(h2, gate, w_eg, w_eu, w_ed, x1, gt)


def _layer_rows(x, mods, wts, tm, conv_fn, attn_fn):
    sh1, sc1, gt1, sh2, sc2, gt2 = mods
    c_ch = wts["w_a"].shape[1]
    h = norm_mod(x, wts["g_n1"], sc1, sh1, BF16, tm)
    u = glu_proj(h, wts["w_a"], wts["w_b"], tm)
    (q,) = proj(h, wts["w_q"], (BF16,), tm)
    k32, k16 = proj(h, wts["w_k"], (F32, BF16), tm)
    v32, v16 = proj(h, wts["w_v"], (F32, BF16), tm)
    act, new_conv = conv_fn(u)
    att = attn_fn(q, k16, v16)
    m = merge(h, act, att, wts["w_ga"], wts["w_gb"], wts["w_co"], wts["w_ao"], tm)
    x1, h2, gate = oproj_router(m, wts["w_o"], x, gt1, wts["g_n2"], sc2, sh2,
                                wts["w_rg"], wts["b_rg"], wts["w_re"], wts["b_re"], min(tm, 256))
    x2 = moe_dense(h2, gate, wts["w_eg"], wts["w_eu"], wts["w_ed"], x1, gt2, min(tm, 512))
    del c_ch
    return x2, k32, v32, new_conv


def kernel(x_prompt, x_sample, cache_k, cache_v, state_conv, page_table, c_prompt, c_sample, w_ada, b_ada, g_n1, w_in, b_sb, w_dw, b_dw, ln_g, ln_b, w_conv_out, w_attn_out, w_o, g_n2, w_rg, b_rg, w_re, b_re, w_eg, w_eu, w_ed, g_final):
    batch, seq, d = x_prompt.shape
    n_seq, t_new, _ = x_sample.shape
    assert batch == 1 and t_new == 1, "one prompt sequence and one new token per sample sequence"
    n_layers = w_in.shape[0]
    c_ch = w_dw.shape[-1]
    attn_w = N_HEADS * HEAD_DIM
    hist = CONV_KERNEL - 1

    mod_rows = n_seq + 16
    c_all = jnp.zeros((mod_rows, d), F32).at[:n_seq].set(c_sample).at[n_seq:n_seq + 1].set(c_prompt)
    mod = ada_mod(c_all, w_ada, b_ada)

    xp = x_prompt.reshape(seq, d)
    xs = x_sample.reshape(n_seq, d)
    tm_p = min(1024, seq)
    outs = {name: [] for name in ("kp", "vp", "cp", "ks", "vs", "cs")}
    zero_halo = jnp.zeros((HALO, c_ch), F32)
    for l in range(n_layers):
        w_in_l = w_in[l].astype(BF16)
        o = 2 * c_ch
        wts = dict(
            g_n1=g_n1[l], g_n2=g_n2[l],
            w_a=w_in_l[:, :c_ch], w_b=w_in_l[:, c_ch:o],
            w_q=w_in_l[:, o:o + attn_w], w_k=w_in_l[:, o + attn_w:o + 2 * attn_w],
            w_v=w_in_l[:, o + 2 * attn_w:o + 3 * attn_w],
            w_ga=w_in_l[:, o + 3 * attn_w:o + 3 * attn_w + d], w_gb=w_in_l[:, o + 3 * attn_w + d:],
            w_co=w_conv_out[l].astype(BF16), w_ao=w_attn_out[l].astype(BF16), w_o=w_o[l].astype(BF16),
            w_rg=w_rg[l], b_rg=b_rg[l], w_re=w_re[l], b_re=b_re[l],
            w_eg=w_eg[l].astype(BF16), w_eu=w_eu[l].astype(BF16), w_ed=w_ed[l].astype(BF16),
        )
        mods_s = tuple(mod[l, :n_seq, i * d:(i + 1) * d] for i in range(6))
        mods_p = tuple(mod[l, n_seq:n_seq + 1, i * d:(i + 1) * d] for i in range(6))

        def conv_p(u, l=l):
            act = conv_prompt(u, zero_halo, w_dw[l], b_dw[l], ln_g[l], ln_b[l])
            return act, u[seq - hist:]

        def attn_p(q, k16, v16, l=l):
            return attn_prompt(q, k16, v16, b_sb[l])

        def conv_s(u, l=l):
            return conv_sample(u, state_conv[l], w_dw[l], b_dw[l], ln_g[l], ln_b[l])

        def attn_s(q, k16, v16, l=l):
            att = attn_sample(q.reshape(n_seq, N_HEADS, HEAD_DIM), cache_k, cache_v, l, page_table, b_sb[l])
            return att.reshape(n_seq, attn_w)

        xp, kp, vp, cp = _layer_rows(xp, mods_p, wts, tm_p, conv_p, attn_p)
        xs, ks, vs, cs = _layer_rows(xs, mods_s, wts, n_seq, conv_s, attn_s)
        outs["kp"].append(kp.reshape(1, seq, N_HEADS, HEAD_DIM))
        outs["vp"].append(vp.reshape(1, seq, N_HEADS, HEAD_DIM))
        outs["cp"].append(cp.reshape(1, hist, c_ch))
        outs["ks"].append(ks.reshape(n_seq, 1, N_HEADS, HEAD_DIM))
        outs["vs"].append(vs.reshape(n_seq, 1, N_HEADS, HEAD_DIM))
        outs["cs"].append(cs)

    zeros_row = jnp.zeros((1, d), F32)
    y_prompt = norm_mod(xp, g_final, zeros_row, zeros_row, F32, min(512, seq)).reshape(1, seq, d)
    y_sample = norm_mod(xs, g_final, zeros_row, zeros_row, F32, n_seq).reshape(n_seq, 1, d)
    return (y_prompt, y_sample, jnp.stack(outs["kp"]), jnp.stack(outs["vp"]), jnp.stack(outs["cp"]),
            jnp.stack(outs["ks"]), jnp.stack(outs["vs"]), jnp.stack(outs["cs"]))
```

```python
import functools

import jax
import jax.numpy as jnp
from jax import lax
from jax.experimental import pallas as pl
from jax.experimental.pallas import tpu as pltpu

F32 = jnp.float32
BF16 = jnp.bfloat16

N_HEADS = 8
HEAD_DIM = 128
CONV_KERNEL = 31
N_GROUPS = 4
EXPERTS_PER_GROUP = 8
N_EXPERTS = N_GROUPS * EXPERTS_PER_GROUP
EPS = 1e-6

LOG2E = 1.4426950408889634
Q_SCALE = HEAD_DIM ** -0.5 * LOG2E

LANES = 128
BF16_ROWS = 16
MXU_DIM = 256
V7X_VMEM_BYTES = 64 * 1024 * 1024
VMEM_LIMIT = V7X_VMEM_BYTES - 8 * 1024 * 1024
VMEM_LIMIT_MOE = V7X_VMEM_BYTES - 4 * 1024 * 1024


def _cparams(*sem, vmem=VMEM_LIMIT):
    return pltpu.CompilerParams(dimension_semantics=sem, vmem_limit_bytes=vmem)


def _dot(a, b):
    return jnp.dot(a, b, preferred_element_type=F32)


def _dot_nt(a, b):
    return lax.dot_general(a, b, (((1,), (1,)), ((), ())), preferred_element_type=F32)


def _silu(x):
    return x * jax.nn.sigmoid(x)


def _round_up(x, m):
    return (x + m - 1) // m * m


def _row_spec(rows, tm, width):
    if rows == 1:
        return pl.BlockSpec((1, width), lambda i, *_: (0, 0))
    return pl.BlockSpec((tm, width), lambda i, *_: (i, 0))


def _ada_kernel(c_ref, w_ref, b_ref, o_ref):
    a = _silu(c_ref[...]).astype(BF16)
    o_ref[0] = _dot(a, w_ref[0].astype(BF16)) + b_ref[0]


def ada_mod(c_all, w_ada, b_ada, tn=512):
    n_layers, d, n = w_ada.shape
    rows = c_all.shape[0]
    return pl.pallas_call(
        _ada_kernel,
        grid=(n_layers, n // tn),
        in_specs=[
            pl.BlockSpec((rows, d), lambda l, j: (0, 0)),
            pl.BlockSpec((1, d, tn), lambda l, j: (l, 0, j)),
            pl.BlockSpec((1, 1, tn), lambda l, j: (l, 0, j)),
        ],
        out_specs=pl.BlockSpec((1, rows, tn), lambda l, j: (l, 0, j)),
        out_shape=jax.ShapeDtypeStruct((n_layers, rows, n), F32),
        compiler_params=_cparams("parallel", "parallel"),
        name="ada_mod",
    )(c_all, w_ada, b_ada.reshape(n_layers, 1, n))


def _rms_mod(x, g, sc, sh):
    y = x * lax.rsqrt(jnp.mean(x * x, axis=-1, keepdims=True) + EPS)
    return (y * g) * (1.0 + sc) + sh


def _norm_mod_kernel(x_ref, g_ref, sc_ref, sh_ref, o_ref):
    o_ref[...] = _rms_mod(x_ref[...], g_ref[...], sc_ref[...], sh_ref[...]).astype(o_ref.dtype)


def norm_mod(x, g, sc, sh, out_dtype, tm):
    t, d = x.shape
    return pl.pallas_call(
        _norm_mod_kernel,
        grid=(t // tm,),
        in_specs=[
            pl.BlockSpec((tm, d), lambda i: (i, 0)),
            pl.BlockSpec((1, d), lambda i: (0, 0)),
            _row_spec(sc.shape[0], tm, d),
            _row_spec(sh.shape[0], tm, d),
        ],
        out_specs=pl.BlockSpec((tm, d), lambda i: (i, 0)),
        out_shape=jax.ShapeDtypeStruct((t, d), out_dtype),
        compiler_params=_cparams("parallel"),
        name="norm_mod",
    )(x, g.reshape(1, d), sc, sh)


def _resid_norm_kernel(x_ref, m_ref, gt_ref, g_ref, sc_ref, sh_ref, x2_ref, o_ref):
    x = x_ref[...] + gt_ref[...] * m_ref[...]
    x2_ref[...] = x
    o_ref[...] = _rms_mod(x, g_ref[...], sc_ref[...], sh_ref[...]).astype(o_ref.dtype)


def resid_norm(x, m, gt, g, sc, sh, out_dtype, tm):
    t, d = x.shape
    return pl.pallas_call(
        _resid_norm_kernel,
        grid=(t // tm,),
        in_specs=[
            pl.BlockSpec((tm, d), lambda i: (i, 0)),
            pl.BlockSpec((tm, d), lambda i: (i, 0)),
            _row_spec(gt.shape[0], tm, d),
            pl.BlockSpec((1, d), lambda i: (0, 0)),
            _row_spec(sc.shape[0], tm, d),
            _row_spec(sh.shape[0], tm, d),
        ],
        out_specs=[pl.BlockSpec((tm, d), lambda i: (i, 0)), pl.BlockSpec((tm, d), lambda i: (i, 0))],
        out_shape=[jax.ShapeDtypeStruct((t, d), F32), jax.ShapeDtypeStruct((t, d), out_dtype)],
        compiler_params=_cparams("parallel"),
        name="resid_norm",
    )(x, m, gt, g.reshape(1, d), sc, sh)


def _glu_kernel(h_ref, wa_ref, wb_ref, u_ref):
    h = h_ref[...]
    u_ref[...] = _dot(h, wa_ref[...]) * jax.nn.sigmoid(_dot(h, wb_ref[...]))


def glu_proj(h, wa, wb, tm, tn=512):
    t, d = h.shape
    n = wa.shape[1]
    return pl.pallas_call(
        _glu_kernel,
        grid=(t // tm, n // tn),
        in_specs=[
            pl.BlockSpec((tm, d), lambda i, j: (i, 0)),
            pl.BlockSpec((d, tn), lambda i, j: (0, j)),
            pl.BlockSpec((d, tn), lambda i, j: (0, j)),
        ],
        out_specs=pl.BlockSpec((tm, tn), lambda i, j: (i, j)),
        out_shape=jax.ShapeDtypeStruct((t, n), F32),
        compiler_params=_cparams("parallel", "parallel"),
        name="glu_proj",
    )(h, wa, wb)


def _proj_kernel(h_ref, w_ref, *o_refs, scale):
    r = _dot(h_ref[...], w_ref[...])
    if scale != 1.0:
        r = r * scale
    for o_ref in o_refs:
        o_ref[...] = r.astype(o_ref.dtype)


def proj(h, w, out_dtypes, tm, tn=512, scale=1.0):
    t, d = h.shape
    n = w.shape[1]
    return pl.pallas_call(
        functools.partial(_proj_kernel, scale=scale),
        grid=(t // tm, n // tn),
        in_specs=[
            pl.BlockSpec((tm, d), lambda i, j: (i, 0)),
            pl.BlockSpec((d, tn), lambda i, j: (0, j)),
        ],
        out_specs=[pl.BlockSpec((tm, tn), lambda i, j: (i, j)) for _ in out_dtypes],
        out_shape=[jax.ShapeDtypeStruct((t, n), dt) for dt in out_dtypes],
        compiler_params=_cparams("parallel", "parallel"),
        name="proj",
    )(h, w)


def _layer_norm_silu(dc, g, b):
    mu = jnp.mean(dc, axis=-1, keepdims=True)
    xc = dc - mu
    var = jnp.mean(xc * xc, axis=-1, keepdims=True)
    return _silu((xc * lax.rsqrt(var + EPS)) * g + b)


HALO = 32


def _conv_prompt_kernel(u_ref, halo_ref, prev_ref, w_ref, b_ref, g_ref, beta_ref, o_ref, buf_ref, dc_ref):
    i = pl.program_id(0)
    tt, c = u_ref.shape

    @pl.when(i == 0)
    def _():
        buf_ref[0:HALO, :] = prev_ref[...]

    @pl.when(i > 0)
    def _():
        buf_ref[0:HALO, :] = halo_ref[...]

    buf_ref[HALO:HALO + tt, :] = u_ref[...]
    first = HALO - (CONV_KERNEL - 1)
    for cc in range(c // LANES):
        ls = slice(cc * LANES, (cc + 1) * LANES)
        acc = jnp.broadcast_to(b_ref[:, ls], (tt, LANES))
        for j in range(CONV_KERNEL):
            acc = acc + w_ref[j:j + 1, ls] * buf_ref[first + j:first + j + tt, ls]
        dc_ref[:, ls] = acc
    o_ref[...] = _layer_norm_silu(dc_ref[...], g_ref[...], beta_ref[...]).astype(o_ref.dtype)


def conv_prompt(u, prev_halo, w_dw, b_dw, ln_g, ln_b, tt=256):
    t, c = u.shape
    ratio = tt // HALO
    return pl.pallas_call(
        _conv_prompt_kernel,
        grid=(t // tt,),
        in_specs=[
            pl.BlockSpec((tt, c), lambda i: (i, 0)),
            pl.BlockSpec((HALO, c), lambda i: (jnp.maximum(i * ratio - 1, 0), 0)),
            pl.BlockSpec((HALO, c), lambda i: (0, 0)),
            pl.BlockSpec((CONV_KERNEL, c), lambda i: (0, 0)),
            pl.BlockSpec((1, c), lambda i: (0, 0)),
            pl.BlockSpec((1, c), lambda i: (0, 0)),
            pl.BlockSpec((1, c), lambda i: (0, 0)),
        ],
        out_specs=pl.BlockSpec((tt, c), lambda i: (i, 0)),
        out_shape=jax.ShapeDtypeStruct((t, c), BF16),
        scratch_shapes=[pltpu.VMEM((HALO + tt, c), F32), pltpu.VMEM((tt, c), F32)],
        compiler_params=_cparams("arbitrary"),
        name="conv_prompt",
    )(u, u, prev_halo, w_dw, b_dw.reshape(1, c), ln_g.reshape(1, c), ln_b.reshape(1, c))


def _conv_sample_kernel(u_ref, prev_ref, w_ref, b_ref, g_ref, beta_ref, act_ref, new_ref, dc_ref):
    bb = u_ref.shape[0]
    hist = CONV_KERNEL - 1
    w_hist = w_ref[0:hist, :]
    w_last = w_ref[hist:hist + 1, :]
    for b in range(bb):
        prev = prev_ref[b]
        u_row = u_ref[b:b + 1, :]
        dc_ref[b:b + 1, :] = (jnp.sum(prev * w_hist, axis=0, keepdims=True) + w_last * u_row + b_ref[...])
        new_ref[b, 0:hist - 1, :] = prev_ref[b, 1:hist, :]
        new_ref[b, hist - 1:hist, :] = u_row
    act_ref[...] = _layer_norm_silu(dc_ref[...], g_ref[...], beta_ref[...]).astype(act_ref.dtype)


def conv_sample(u, prev, w_dw, b_dw, ln_g, ln_b, bb=8):
    n, c = u.shape
    hist = prev.shape[1]
    return pl.pallas_call(
        _conv_sample_kernel,
        grid=(n // bb,),
        in_specs=[
            pl.BlockSpec((bb, c), lambda i: (i, 0)),
            pl.BlockSpec((bb, hist, c), lambda i: (i, 0, 0)),
            pl.BlockSpec((CONV_KERNEL, c), lambda i: (0, 0)),
            pl.BlockSpec((1, c), lambda i: (0, 0)),
            pl.BlockSpec((1, c), lambda i: (0, 0)),
            pl.BlockSpec((1, c), lambda i: (0, 0)),
        ],
        out_specs=[
            pl.BlockSpec((bb, c), lambda i: (i, 0)),
            pl.BlockSpec((bb, hist, c), lambda i: (i, 0, 0)),
        ],
        out_shape=[jax.ShapeDtypeStruct((n, c), BF16), jax.ShapeDtypeStruct((n, hist, c), F32)],
        scratch_shapes=[pltpu.VMEM((bb, c), F32)],
        compiler_params=_cparams("parallel"),
        name="conv_sample",
    )(u, prev, w_dw, b_dw.reshape(1, c), ln_g.reshape(1, c), ln_b.reshape(1, c))


def _softplus2(z2):
    sign_bit = jnp.uint32(0x80000000)
    neg_abs = lax.bitcast_convert_type(lax.bitcast_convert_type(z2, jnp.uint32) | sign_bit, F32)
    return jnp.maximum(z2, 0.0) + jnp.log(1.0 + jnp.exp2(neg_abs)) * LOG2E


def _strict_lower(n):
    return (lax.broadcasted_iota(jnp.int32, (n, n), 0) > lax.broadcasted_iota(jnp.int32, (n, n), 1)).astype(BF16)


def _attn_prompt_kernel(bias_ref, q_ref, k_ref, v_ref, o_ref, acc_ref, carry_ref, *, tk, heads):
    qi = pl.program_id(1)
    tq = q_ref.shape[0]
    n_diag = tq // tk
    upper = _strict_lower(tk)
    acc_ref[...] = jnp.zeros_like(acc_ref)
    carry_ref[...] = jnp.zeros_like(carry_ref)

    def tile(hh, kb, masked):
        cols = slice(hh * HEAD_DIM, (hh + 1) * HEAD_DIM)
        start = pl.multiple_of(kb * tk, tk)
        ks = k_ref[pl.ds(start, tk), cols]
        vs = v_ref[pl.ds(start, tk), cols]
        z = _dot_nt(q_ref[:, cols], ks) + bias_ref[hh]
        sp = _softplus2(z)
        ls = z - sp
        if masked:
            t_pos = qi * tq + lax.broadcasted_iota(jnp.int32, (tq, tk), 0)
            s_pos = kb * tk + lax.broadcasted_iota(jnp.int32, (tq, tk), 1)
            valid = s_pos < t_pos
            sp = jnp.where(valid, sp, 0.0)
        spb = sp.astype(BF16)
        later = _dot(spb, upper)
        carry = carry_ref[hh]
        a = jnp.exp2(ls - later - carry)
        if masked:
            a = jnp.where(valid, a, 0.0)
        acc_ref[hh] += _dot(a.astype(BF16), vs)
        carry_ref[hh] = carry + later[:, 0:1] + spb[:, 0:1].astype(F32)

    for d in range(n_diag):
        for hh in range(heads):
            tile(hh, qi * n_diag + (n_diag - 1 - d), True)

    def body(j, c):
        for hh in range(heads):
            tile(hh, qi * n_diag - 1 - j, False)
        return c

    lax.fori_loop(0, qi * n_diag, body, 0)
    for hh in range(heads):
        o_ref[:, hh * HEAD_DIM:(hh + 1) * HEAD_DIM] = acc_ref[hh].astype(o_ref.dtype)


def attn_prompt(q, k, v, b_sb, tq=512, tk=MXU_DIM, heads=2):
    t = q.shape[0]
    tq = min(tq, t)
    width = heads * HEAD_DIM
    bias = jnp.broadcast_to((b_sb.astype(F32) * LOG2E)[:, None, None], (N_HEADS, 1, tk))
    kern = functools.partial(_attn_prompt_kernel, tk=tk, heads=heads)
    return pl.pallas_call(
        kern,
        grid=(N_HEADS // heads, t // tq),
        in_specs=[
            pl.BlockSpec((heads, 1, tk), lambda g, i: (g, 0, 0)),
            pl.BlockSpec((tq, width), lambda g, i: (i, g)),
            pl.BlockSpec((t, width), lambda g, i: (0, g)),
            pl.BlockSpec((t, width), lambda g, i: (0, g)),
        ],
        out_specs=pl.BlockSpec((tq, width), lambda g, i: (i, g)),
        out_shape=jax.ShapeDtypeStruct(q.shape, BF16),
        scratch_shapes=[pltpu.VMEM((heads, tq, HEAD_DIM), F32), pltpu.VMEM((heads, tq, 1), F32)],
        compiler_params=_cparams("parallel", "arbitrary"),
        name="attn_prompt",
    )(bias, q, k, v)


def _attn_sample_kernel(pt_ref, bias_ref, q_ref, *refs, pp, page):
    k_refs, v_refs = refs[:pp], refs[pp:2 * pp]
    o_ref, acc_ref, carry_ref = refs[2 * pp:]
    j = pl.program_id(1)
    nh = N_HEADS

    @pl.when(j == 0)
    def _():
        acc_ref[...] = jnp.zeros_like(acc_ref)
        carry_ref[...] = jnp.zeros_like(carry_ref)

    q = q_ref[0]
    head = lax.broadcasted_iota(jnp.int32, (nh, LANES), 0)

    def head_rows(page_refs, h):
        return jnp.concatenate(
            [r[0, 0, pl.ds(h, page, stride=nh), :] for r in page_refs], axis=0).astype(BF16)

    z_pages = [jnp.zeros((nh, page), F32) for _ in range(pp)]
    for h in range(nh):
        z_h = _dot_nt(q, head_rows(k_refs, h))
        z_pages = [jnp.where(head == h, z_h[:, p * page:(p + 1) * page], z_pages[p]) for p in range(pp)]
    z = jnp.concatenate(z_pages, axis=0) + bias_ref[...]
    sp = _softplus2(z)
    spb = sp.astype(BF16)
    later = _dot(spb, _strict_lower(page))
    total = later[:, 0:1] + spb[:, 0:1].astype(F32)
    carry = carry_ref[...]
    carries = [None] * pp
    for p in reversed(range(pp)):
        carries[p] = carry
        carry = carry + total[p * nh:(p + 1) * nh]
    carry_ref[...] = carry
    a = jnp.exp2((z - sp) - later - jnp.concatenate(carries, axis=0))
    a_cat = jnp.concatenate([a[p * nh:(p + 1) * nh] for p in range(pp)], axis=1).astype(BF16)
    acc = acc_ref[...]
    for h in range(nh):
        acc = jnp.where(head == h, acc + _dot(a_cat, head_rows(v_refs, h)), acc)
    acc_ref[...] = acc

    @pl.when(j == pl.num_programs(1) - 1)
    def _():
        o_ref[0] = acc.astype(o_ref.dtype)


def attn_sample(q, cache_k, cache_v, layer, page_table, b_sb, pp=8):
    n_seq, n_pages = page_table.shape
    n_layers, n_pool, page = cache_k.shape[:3]
    pp = min(pp, n_pages)
    steps = n_pages // pp
    ck = cache_k.reshape(n_layers, n_pool, page * N_HEADS, HEAD_DIM)
    cv = cache_v.reshape(n_layers, n_pool, page * N_HEADS, HEAD_DIM)
    bias = jnp.broadcast_to((b_sb.astype(F32) * LOG2E)[None, :, None], (pp, N_HEADS, page)).reshape(pp * N_HEADS, page)

    def page_spec(p):
        def index(b, j, pt):
            return (layer, pt[b * n_pages + (steps - 1 - j) * pp + p], 0, 0)
        return pl.BlockSpec((1, 1, page * N_HEADS, HEAD_DIM), index)

    kern = functools.partial(_attn_sample_kernel, pp=pp, page=page)
    grid_spec = pltpu.PrefetchScalarGridSpec(
        num_scalar_prefetch=1,
        grid=(n_seq, steps),
        in_specs=[
            pl.BlockSpec((pp * N_HEADS, page), lambda b, j, pt: (0, 0)),
            pl.BlockSpec((1, N_HEADS, HEAD_DIM), lambda b, j, pt: (b, 0, 0)),
            *[page_spec(p) for p in range(pp)],
            *[page_spec(p) for p in range(pp)],
        ],
        out_specs=pl.BlockSpec((1, N_HEADS, HEAD_DIM), lambda b, j, pt: (b, 0, 0)),
        scratch_shapes=[pltpu.VMEM((N_HEADS, HEAD_DIM), F32), pltpu.VMEM((N_HEADS, 1), F32)],
    )
    return pl.pallas_call(
        kern,
        grid_spec=grid_spec,
        out_shape=jax.ShapeDtypeStruct((n_seq, N_HEADS, HEAD_DIM), BF16),
        compiler_params=_cparams("parallel", "arbitrary"),
        name="attn_sample",
    )(page_table.reshape(-1), bias, q, *([ck] * pp), *([cv] * pp))


def _merge_kernel(h_ref, act_ref, att_ref, wga_ref, wgb_ref, wco_ref, wao_ref, o_ref):
    h = h_ref[...]
    conv_out = _dot(act_ref[...], wco_ref[...])
    att_out = _dot(att_ref[...], wao_ref[...])
    ga = jax.nn.sigmoid(_dot(h, wga_ref[...]))
    gb = jax.nn.sigmoid(_dot(h, wgb_ref[...]))
    o_ref[...] = (ga * conv_out + gb * att_out).astype(o_ref.dtype)


def merge(h, act, att, wga, wgb, wco, wao, tm, tn=512):
    t, d = h.shape
    c = act.shape[1]
    n = wga.shape[1]
    return pl.pallas_call(
        _merge_kernel,
        grid=(t // tm, n // tn),
        in_specs=[
            pl.BlockSpec((tm, d), lambda i, j: (i, 0)),
            pl.BlockSpec((tm, c), lambda i, j: (i, 0)),
            pl.BlockSpec((tm, c), lambda i, j: (i, 0)),
            pl.BlockSpec((d, tn), lambda i, j: (0, j)),
            pl.BlockSpec((d, tn), lambda i, j: (0, j)),
            pl.BlockSpec((c, tn), lambda i, j: (0, j)),
            pl.BlockSpec((c, tn), lambda i, j: (0, j)),
        ],
        out_specs=pl.BlockSpec((tm, tn), lambda i, j: (i, j)),
        out_shape=jax.ShapeDtypeStruct((t, n), BF16),
        compiler_params=_cparams("parallel", "parallel"),
        name="merge",
    )(h, act, att, wga, wgb, wco, wao)


def _first_argmax(x, m, lane):
    return jnp.min(jnp.where(x == m, lane, float(LANES)), axis=-1, keepdims=True)


def _router(lg, le):
    lane = lax.broadcasted_iota(jnp.int32, lg.shape, 1).astype(F32)
    neg = -jnp.inf
    lg = jnp.where(lane < N_GROUPS, lg, neg)
    mg = jnp.max(lg, axis=-1, keepdims=True)
    p_grp = 1.0 / jnp.sum(jnp.exp(lg - mg), axis=-1, keepdims=True)
    g_idx = _first_argmax(lg, mg, lane)
    in_group = jnp.floor(lane * (1.0 / EXPERTS_PER_GROUP)) == g_idx
    le = jnp.where(in_group, le, neg)
    m1 = jnp.max(le, axis=-1, keepdims=True)
    i1 = _first_argmax(le, m1, lane)
    le2 = jnp.where(lane == i1, neg, le)
    m2 = jnp.max(le2, axis=-1, keepdims=True)
    i2 = _first_argmax(le2, m2, lane)
    e2 = jnp.exp(m2 - m1)
    w1 = p_grp / (1.0 + e2)
    w2 = w1 * e2
    gate = jnp.where(lane == i1, w1, 0.0) + jnp.where(lane == i2, w2, 0.0)
    route = jnp.where(lane == 0, i1, jnp.where(lane == 1, i2, jnp.where(lane == 2, w1, jnp.where(lane == 3, w2, 0.0))))
    return gate, route


def _split_dot(x_hi, x_lo, w_hi_ref, w_lo_ref):
    w_hi = w_hi_ref[...]
    return _dot(x_hi, w_hi) + (_dot(x_lo, w_hi) + _dot(x_hi, w_lo_ref[...]))


def _oproj_kernel(m_ref, wo_ref, x_ref, gt_ref, g2_ref, sc_ref, sh_ref, wgh_ref, wgl_ref, weh_ref, wel_ref,
                  bg_ref, be_ref, x1_ref, h2_ref, gate_ref, route_ref):
    x1 = x_ref[...] + gt_ref[...] * _dot(m_ref[...], wo_ref[...])
    x1_ref[...] = x1
    h2 = _rms_mod(x1, g2_ref[...], sc_ref[...], sh_ref[...])
    hi = h2.astype(BF16)
    h2_ref[...] = hi
    lo = (h2 - hi.astype(F32)).astype(BF16)
    lg = _split_dot(hi, lo, wgh_ref, wgl_ref) + bg_ref[...]
    le = _split_dot(hi, lo, weh_ref, wel_ref) + be_ref[...]
    gate_ref[...], route_ref[...] = _router(lg, le)


def _pad_lanes(w):
    return jnp.pad(w, ((0, 0), (0, LANES - w.shape[-1])))


def _hi_lo(w):
    hi = w.astype(BF16)
    return hi, (w - hi.astype(F32)).astype(BF16)


def oproj_router(m, wo, x, gt, g2, sc, sh, w_rg, b_rg, w_re, b_re, tm):
    t, d = x.shape
    wgh, wgl = _hi_lo(_pad_lanes(w_rg))
    weh, wel = _hi_lo(_pad_lanes(w_re))
    bg = _pad_lanes(b_rg.reshape(1, -1))
    be = _pad_lanes(b_re.reshape(1, -1))
    full = lambda shape: pl.BlockSpec(shape, lambda i: (0, 0))
    return pl.pallas_call(
        _oproj_kernel,
        grid=(t // tm,),
        in_specs=[
            pl.BlockSpec((tm, d), lambda i: (i, 0)),
            full((d, d)),
            pl.BlockSpec((tm, d), lambda i: (i, 0)),
            _row_spec(gt.shape[0], tm, d),
            full((1, d)),
            _row_spec(sc.shape[0], tm, d),
            _row_spec(sh.shape[0], tm, d),
            full((d, LANES)), full((d, LANES)), full((d, LANES)), full((d, LANES)),
            full((1, LANES)), full((1, LANES)),
        ],
        out_specs=[
            pl.BlockSpec((tm, d), lambda i: (i, 0)),
            pl.BlockSpec((tm, d), lambda i: (i, 0)),
            pl.BlockSpec((tm, LANES), lambda i: (i, 0)),
            pl.BlockSpec((tm, LANES), lambda i: (i, 0)),
        ],
        out_shape=[
            jax.ShapeDtypeStruct((t, d), F32),
            jax.ShapeDtypeStruct((t, d), BF16),
            jax.ShapeDtypeStruct((t, LANES), F32),
            jax.ShapeDtypeStruct((t, LANES), F32),
        ],
        compiler_params=_cparams("parallel"),
        name="oproj_router",
    )(m, wo, x, gt, g2.reshape(1, d), sc, sh, wgh, wgl, weh, wel, bg, be)


def _moe_kernel(h_ref, gate_ref, wg_ref, wu_ref, wd_ref, x_ref, gt_ref, o_ref, acc_ref):
    e = pl.program_id(1)

    @pl.when(e == 0)
    def _():
        acc_ref[...] = jnp.zeros_like(acc_ref)

    h = h_ref[...]
    gate = gate_ref[...]
    lane = lax.broadcasted_iota(jnp.int32, gate.shape, 1)
    ge = jnp.sum(jnp.where(lane == e, gate, 0.0), axis=-1, keepdims=True)
    act = (_silu(_dot(h, wg_ref[0])) * _dot(h, wu_ref[0])) * ge
    acc_ref[...] += _dot(act.astype(BF16), wd_ref[0])

    @pl.when(e == pl.num_programs(1) - 1)
    def _():
        o_ref[...] = x_ref[...] + gt_ref[...] * acc_ref[...]


def moe_dense(h2, gate, w_eg, w_eu, w_ed, x1, gt, tm):
    t, d = x1.shape
    n_e, _, f = w_eg.shape
    return pl.pallas_call(
        _moe_kernel,
        grid=(t // tm, n_e),
        in_specs=[
            pl.BlockSpec((tm, d), lambda i, e: (i, 0)),
            pl.BlockSpec((tm, LANES), lambda i, e: (i, 0)),
            pl.BlockSpec((1, d, f), lambda i, e: (e, 0, 0)),
            pl.BlockSpec((1, d, f), lambda i, e: (e, 0, 0)),
            pl.BlockSpec((1, f, d), lambda i, e: (e, 0, 0)),
            pl.BlockSpec((tm, d), lambda i, e: (i, 0)),
            _row_spec(gt.shape[0], tm, d),
        ],
        out_specs=pl.BlockSpec((tm, d), lambda i, e: (i, 0)),
        out_shape=jax.ShapeDtypeStruct((t, d), F32),
        scratch_shapes=[pltpu.VMEM((tm, d), F32)],
        compiler_params=_cparams("parallel", "arbitrary"),
        name="moe_dense",
    )(h2, gate, w_eg, w_eu, w_ed, x1, gt)


SEG_ALIGN = BF16_ROWS
EXPERT_ROWS = 128
SORT_ROWS = MXU_DIM


def _routing_tables(route, tm, rows):
    t = route.shape[0]
    nt = t // tm
    e_idx = route[:, 0:2].astype(jnp.int32).reshape(nt, 2 * tm)
    w = route[:, 2:4].reshape(nt, 2 * tm)
    onehot = (e_idx[..., None] == jnp.arange(N_EXPERTS, dtype=jnp.int32)).astype(jnp.int32)
    rank = jnp.sum((jnp.cumsum(onehot, axis=1) - onehot) * onehot, axis=-1)
    cnt = jnp.sum(onehot, axis=1)
    cnt_pad = (cnt + SEG_ALIGN - 1) // SEG_ALIGN * SEG_ALIGN
    start = jnp.cumsum(cnt_pad, axis=1) - cnt_pad
    pos = jnp.take_along_axis(start, e_idx, axis=1) + rank
    tile_ix = jnp.arange(nt, dtype=jnp.int32)[:, None]
    tok = jnp.broadcast_to(jnp.arange(2 * tm, dtype=jnp.int32) // 2, (nt, 2 * tm))
    src = jnp.full((nt, rows), -1, jnp.int32).at[tile_ix, pos].set(tok)
    w_row = jnp.zeros((nt, rows), F32).at[tile_ix, pos].set(w)
    table = jnp.stack([src.astype(F32), w_row], axis=-1)
    trips = (cnt_pad + EXPERT_ROWS - 1) // EXPERT_ROWS
    return (start.reshape(-1), (start + cnt_pad).reshape(-1), trips.reshape(-1), table, src.reshape(nt, 1, rows))


def _moe_routed_kernel(start_ref, end_ref, trips_ref, h_ref, tab_ref, src_ref, wg_ref, wu_ref, wd_ref, o_ref, xy_ref):
    i = pl.program_id(0)
    e = pl.program_id(1)
    n_e = pl.num_programs(1)
    tm = h_ref.shape[0]
    rows = xy_ref.shape[0]

    @pl.when(e == 0)
    def _():
        def sort(c, carry):
            r0 = pl.multiple_of(c * SORT_ROWS, SORT_ROWS)
            token = lax.broadcasted_iota(jnp.int32, (SORT_ROWS, tm), 1).astype(F32)
            pick = jnp.where(tab_ref[0, pl.ds(r0, SORT_ROWS), 0:1] == token, 1.0, 0.0).astype(BF16)
            xy_ref[pl.ds(r0, SORT_ROWS), :] = _dot(pick, h_ref[...]).astype(BF16)
            return carry
        lax.fori_loop(0, rows // SORT_ROWS, sort, 0)

    seg = i * n_e + e
    seg_start = start_ref[seg]
    seg_end = end_ref[seg]

    def expert(c, carry):
        r0 = pl.multiple_of(seg_start + c * EXPERT_ROWS, SEG_ALIGN)
        xs = xy_ref[pl.ds(r0, EXPERT_ROWS), :]
        w_row = tab_ref[0, pl.ds(r0, EXPERT_ROWS), 1:2]
        act = (_silu(_dot(xs, wg_ref[0])) * _dot(xs, wu_ref[0])) * w_row
        y = _dot(act.astype(BF16), wd_ref[0]).astype(BF16)
        row = r0 + lax.broadcasted_iota(jnp.int32, (EXPERT_ROWS, 1), 0)
        xy_ref[pl.ds(r0, EXPERT_ROWS), :] = jnp.where(row < seg_end, y, xs)
        return carry

    lax.fori_loop(0, trips_ref[seg], expert, 0)

    @pl.when(e == n_e - 1)
    def _():
        def combine(c, carry):
            t0 = pl.multiple_of(c * SORT_ROWS, SORT_ROWS)
            token = t0 + lax.broadcasted_iota(jnp.int32, (SORT_ROWS, rows), 0)
            pick = jnp.where(src_ref[0] == token, 1.0, 0.0).astype(BF16)
            o_ref[pl.ds(t0, SORT_ROWS), :] = _dot(pick, xy_ref[...])
            return carry
        lax.fori_loop(0, tm // SORT_ROWS, combine, 0)


def moe_routed(h2, route, w_eg, w_eu, w_ed, tm):
    t, d = h2.shape
    n_e, _, f = w_eg.shape
    rows = _round_up(2 * tm + n_e * (SEG_ALIGN - 1) + EXPERT_ROWS, SORT_ROWS)
    start, end, trips, table, src_row = _routing_tables(route, tm, rows)
    grid_spec = pltpu.PrefetchScalarGridSpec(
        num_scalar_prefetch=3,
        grid=(t // tm, n_e),
        in_specs=[
            pl.BlockSpec((tm, d), lambda i, e, *_: (i, 0)),
            pl.BlockSpec((1, rows, 2), lambda i, e, *_: (i, 0, 0)),
            pl.BlockSpec((1, 1, rows), lambda i, e, *_: (i, 0, 0)),
            pl.BlockSpec((1, d, f), lambda i, e, *_: (e, 0, 0)),
            pl.BlockSpec((1, d, f), lambda i, e, *_: (e, 0, 0)),
            pl.BlockSpec((1, f, d), lambda i, e, *_: (e, 0, 0)),
        ],
        out_specs=pl.BlockSpec((tm, d), lambda i, e, *_: (i, 0)),
        scratch_shapes=[pltpu.VMEM((rows, d), BF16)],
    )
    return pl.pallas_call(
        _moe_routed_kernel,
        grid_spec=grid_spec,
        out_shape=jax.ShapeDtypeStruct((t, d), F32),
        compiler_params=_cparams("parallel", "arbitrary", vmem=VMEM_LIMIT_MOE),
        name="moe_routed",
    )(start, end, trips, h2, table, src_row, w_eg, w_eu, w_ed)


def _mix_rows(x, h, mods, wts, tm, conv_fn, attn_fn):
    _, _, gt1, sh2, sc2, _ = mods
    u = glu_proj(h, wts["w_a"], wts["w_b"], tm)
    (q,) = proj(h, wts["w_q"], (BF16,), tm, scale=Q_SCALE)
    k32, k16 = proj(h, wts["w_k"], (F32, BF16), tm)
    v32, v16 = proj(h, wts["w_v"], (F32, BF16), tm)
    act, new_conv = conv_fn(u)
    att = attn_fn(q, k16, v16)
    m = merge(h, act, att, wts["w_ga"], wts["w_gb"], wts["w_co"], wts["w_ao"], tm)
    x1, h2, gate, route = oproj_router(m, wts["w_o"], x, gt1, wts["g_n2"], sc2, sh2,
                                       wts["w_rg"], wts["b_rg"], wts["w_re"], wts["b_re"], min(tm, 256))
    return x1, h2, gate, route, k32, v32, new_conv


def kernel(x_prompt, x_sample, cache_k, cache_v, state_conv, page_table, c_prompt, c_sample, w_ada, b_ada, g_n1, w_in, b_sb, w_dw, b_dw, ln_g, ln_b, w_conv_out, w_attn_out, w_o, g_n2, w_rg, b_rg, w_re, b_re, w_eg, w_eu, w_ed, g_final):
    batch, seq, d = x_prompt.shape
    n_seq, t_new, _ = x_sample.shape
    assert batch == 1 and t_new == 1, "one prompt sequence and one new token per sample sequence"
    n_layers = w_in.shape[0]
    c_ch = w_dw.shape[-1]
    attn_w = N_HEADS * HEAD_DIM
    hist = CONV_KERNEL - 1
    assert seq >= hist

    mod_rows = n_seq + BF16_ROWS
    c_all = jnp.zeros((mod_rows, d), F32).at[:n_seq].set(c_sample).at[n_seq:n_seq + 1].set(c_prompt)
    mod = ada_mod(c_all, w_ada, b_ada)

    tm_p = min(1024, seq)
    tm_n = min(512, seq)
    xp = x_prompt.reshape(seq, d)
    xs = x_sample.reshape(n_seq, d)
    pending = None
    outs = {name: [] for name in ("kp", "vp", "cp", "ks", "vs", "cs")}
    zero_halo = jnp.zeros((HALO, c_ch), F32)
    zeros_row = jnp.zeros((1, d), F32)
    for l in range(n_layers):
        w_in_l = w_in[l].astype(BF16)
        o = 2 * c_ch
        wts = dict(
            g_n2=g_n2[l],
            w_a=w_in_l[:, :c_ch], w_b=w_in_l[:, c_ch:o],
            w_q=w_in_l[:, o:o + attn_w], w_k=w_in_l[:, o + attn_w:o + 2 * attn_w],
            w_v=w_in_l[:, o + 2 * attn_w:o + 3 * attn_w],
            w_ga=w_in_l[:, o + 3 * attn_w:o + 3 * attn_w + d], w_gb=w_in_l[:, o + 3 * attn_w + d:],
            w_co=w_conv_out[l].astype(BF16), w_ao=w_attn_out[l].astype(BF16), w_o=w_o[l].astype(BF16),
            w_rg=w_rg[l], b_rg=b_rg[l], w_re=w_re[l], b_re=b_re[l],
        )
        weg, weu, wed = w_eg[l].astype(BF16), w_eu[l].astype(BF16), w_ed[l].astype(BF16)
        mods_s = tuple(mod[l, :n_seq, i * d:(i + 1) * d] for i in range(6))
        mods_p = tuple(mod[l, n_seq:n_seq + 1, i * d:(i + 1) * d] for i in range(6))

        def conv_p(u, l=l):
            act = conv_prompt(u, zero_halo, w_dw[l], b_dw[l], ln_g[l], ln_b[l])
            return act, u[seq - hist:]

        def attn_p(q, k16, v16, l=l):
            return attn_prompt(q, k16, v16, b_sb[l])

        def conv_s(u, l=l):
            return conv_sample(u, state_conv[l], w_dw[l], b_dw[l], ln_g[l], ln_b[l])

        def attn_s(q, k16, v16, l=l):
            att = attn_sample(q.reshape(n_seq, N_HEADS, HEAD_DIM), cache_k, cache_v, l, page_table, b_sb[l])
            return att.reshape(n_seq, attn_w)

        if pending is None:
            hp = norm_mod(xp, g_n1[l], mods_p[1], mods_p[0], BF16, tm_p)
        else:
            xp, hp = resid_norm(*pending, g_n1[l], mods_p[1], mods_p[0], BF16, tm_n)
        x1, h2, _, route, kp, vp, cp = _mix_rows(xp, hp, mods_p, wts, tm_p, conv_p, attn_p)
        pending = (x1, moe_routed(h2, route, weg, weu, wed, tm_p), mods_p[5])

        hs = norm_mod(xs, g_n1[l], mods_s[1], mods_s[0], BF16, n_seq)
        x1s, h2s, gate_s, _, ks, vs, cs = _mix_rows(xs, hs, mods_s, wts, n_seq, conv_s, attn_s)
        xs = moe_dense(h2s, gate_s, weg, weu, wed, x1s, mods_s[5], n_seq)

        outs["kp"].append(kp.reshape(1, seq, N_HEADS, HEAD_DIM))
        outs["vp"].append(vp.reshape(1, seq, N_HEADS, HEAD_DIM))
        outs["cp"].append(cp.reshape(1, hist, c_ch))
        outs["ks"].append(ks.reshape(n_seq, 1, N_HEADS, HEAD_DIM))
        outs["vs"].append(vs.reshape(n_seq, 1, N_HEADS, HEAD_DIM))
        outs["cs"].append(cs)

    _, y_prompt = resid_norm(*pending, g_final, zeros_row, zeros_row, F32, tm_n)
    y_sample = norm_mod(xs, g_final, zeros_row, zeros_row, F32, n_seq)
    return (y_prompt.reshape(1, seq, d), y_sample.reshape(n_seq, 1, d), jnp.stack(outs["kp"]), jnp.stack(outs["vp"]),
            jnp.stack(outs["cp"]), jnp.stack(outs["ks"]), jnp.stack(outs["vs"]), jnp.stack(outs["cs"]))
```

```python
import functools

import jax
import jax.numpy as jnp
from jax import lax
from jax.experimental import pallas as pl
from jax.experimental.pallas import tpu as pltpu

F32 = jnp.float32
BF16 = jnp.bfloat16

N_HEADS = 8
HEAD_DIM = 128
CONV_KERNEL = 31
N_GROUPS = 4
EXPERTS_PER_GROUP = 8
N_EXPERTS = N_GROUPS * EXPERTS_PER_GROUP
EPS = 1e-6

LOG2E = 1.4426950408889634
Q_SCALE = HEAD_DIM ** -0.5 * LOG2E

LANES = 128
SUBLANES = 8
BF16_ROWS = 16
MXU_DIM = 256
V7X_VMEM_BYTES = 64 * 1024 * 1024
VMEM_LIMIT = V7X_VMEM_BYTES - 8 * 1024 * 1024
VMEM_LIMIT_MOE = V7X_VMEM_BYTES - 4 * 1024 * 1024


def _cparams(*sem, vmem=VMEM_LIMIT):
    return pltpu.CompilerParams(dimension_semantics=sem, vmem_limit_bytes=vmem)


def _dot(a, b):
    return jnp.dot(a, b, preferred_element_type=F32)


def _dot_nt(a, b):
    return lax.dot_general(a, b, (((1,), (1,)), ((), ())), preferred_element_type=F32)


def _silu(x):
    return x * jax.nn.sigmoid(x)


def _round_up(x, m):
    return (x + m - 1) // m * m


def _row_spec(rows, tm, width):
    if rows == 1:
        return pl.BlockSpec((1, width), lambda i, *_: (0, 0))
    return pl.BlockSpec((tm, width), lambda i, *_: (i, 0))


def _ada_kernel(c_ref, w_ref, b_ref, o_ref):
    a = _silu(c_ref[...]).astype(BF16)
    o_ref[0] = _dot(a, w_ref[0].astype(BF16)) + b_ref[0]


def ada_mod(c_all, w_ada, b_ada, tn=512):
    n_layers, d, n = w_ada.shape
    rows = c_all.shape[0]
    return pl.pallas_call(
        _ada_kernel,
        grid=(n_layers, n // tn),
        in_specs=[
            pl.BlockSpec((rows, d), lambda l, j: (0, 0)),
            pl.BlockSpec((1, d, tn), lambda l, j: (l, 0, j)),
            pl.BlockSpec((1, 1, tn), lambda l, j: (l, 0, j)),
        ],
        out_specs=pl.BlockSpec((1, rows, tn), lambda l, j: (l, 0, j)),
        out_shape=jax.ShapeDtypeStruct((n_layers, rows, n), F32),
        compiler_params=_cparams("parallel", "parallel"),
        name="ada_mod",
    )(c_all, w_ada, b_ada.reshape(n_layers, 1, n))


def _rms_mod(x, g, sc, sh):
    y = x * lax.rsqrt(jnp.mean(x * x, axis=-1, keepdims=True) + EPS)
    return (y * g) * (1.0 + sc) + sh


def _norm_mod_kernel(x_ref, g_ref, sc_ref, sh_ref, o_ref):
    o_ref[...] = _rms_mod(x_ref[...], g_ref[...], sc_ref[...], sh_ref[...]).astype(o_ref.dtype)


def norm_mod(x, g, sc, sh, out_dtype, tm):
    t, d = x.shape
    return pl.pallas_call(
        _norm_mod_kernel,
        grid=(t // tm,),
        in_specs=[
            pl.BlockSpec((tm, d), lambda i: (i, 0)),
            pl.BlockSpec((1, d), lambda i: (0, 0)),
            _row_spec(sc.shape[0], tm, d),
            _row_spec(sh.shape[0], tm, d),
        ],
        out_specs=pl.BlockSpec((tm, d), lambda i: (i, 0)),
        out_shape=jax.ShapeDtypeStruct((t, d), out_dtype),
        compiler_params=_cparams("parallel"),
        name="norm_mod",
    )(x, g.reshape(1, d), sc, sh)


def _resid_norm_kernel(x_ref, m_ref, gt_ref, g_ref, sc_ref, sh_ref, x2_ref, o_ref):
    x = x_ref[...] + gt_ref[...] * m_ref[...]
    x2_ref[...] = x
    o_ref[...] = _rms_mod(x, g_ref[...], sc_ref[...], sh_ref[...]).astype(o_ref.dtype)


def resid_norm(x, m, gt, g, sc, sh, out_dtype, tm):
    t, d = x.shape
    return pl.pallas_call(
        _resid_norm_kernel,
        grid=(t // tm,),
        in_specs=[
            pl.BlockSpec((tm, d), lambda i: (i, 0)),
            pl.BlockSpec((tm, d), lambda i: (i, 0)),
            _row_spec(gt.shape[0], tm, d),
            pl.BlockSpec((1, d), lambda i: (0, 0)),
            _row_spec(sc.shape[0], tm, d),
            _row_spec(sh.shape[0], tm, d),
        ],
        out_specs=[pl.BlockSpec((tm, d), lambda i: (i, 0)), pl.BlockSpec((tm, d), lambda i: (i, 0))],
        out_shape=[jax.ShapeDtypeStruct((t, d), F32), jax.ShapeDtypeStruct((t, d), out_dtype)],
        compiler_params=_cparams("parallel"),
        name="resid_norm",
    )(x, m, gt, g.reshape(1, d), sc, sh)


def _glu_kernel(h_ref, wa_ref, wb_ref, u_ref):
    h = h_ref[...]
    u_ref[...] = _dot(h, wa_ref[...]) * jax.nn.sigmoid(_dot(h, wb_ref[...]))


def _cols(d, tn, col0):
    first = col0 // tn
    assert first * tn == col0
    return pl.BlockSpec((d, tn), lambda i, j: (0, first + j))


def glu_proj(h, w, col_a, col_b, n, tm, tn=512):
    t, d = h.shape
    return pl.pallas_call(
        _glu_kernel,
        grid=(t // tm, n // tn),
        in_specs=[pl.BlockSpec((tm, d), lambda i, j: (i, 0)), _cols(d, tn, col_a), _cols(d, tn, col_b)],
        out_specs=pl.BlockSpec((tm, tn), lambda i, j: (i, j)),
        out_shape=jax.ShapeDtypeStruct((t, n), F32),
        compiler_params=_cparams("parallel", "parallel"),
        name="glu_proj",
    )(h, w, w)


def _qkv_kernel(h_ref, wq_ref, wk_ref, wv_ref, q_ref, k32_ref, k16_ref, v32_ref, v16_ref):
    h = h_ref[...]
    q_ref[...] = (_dot(h, wq_ref[...]) * Q_SCALE).astype(q_ref.dtype)
    k = _dot(h, wk_ref[...])
    k32_ref[...] = k
    k16_ref[...] = k.astype(k16_ref.dtype)
    v = _dot(h, wv_ref[...])
    v32_ref[...] = v
    v16_ref[...] = v.astype(v16_ref.dtype)


def qkv_proj(h, w, col_q, n, tm, tn=512):
    t, d = h.shape
    out = pl.BlockSpec((tm, tn), lambda i, j: (i, j))
    return pl.pallas_call(
        _qkv_kernel,
        grid=(t // tm, n // tn),
        in_specs=[pl.BlockSpec((tm, d), lambda i, j: (i, 0)),
                  _cols(d, tn, col_q), _cols(d, tn, col_q + n), _cols(d, tn, col_q + 2 * n)],
        out_specs=[out] * 5,
        out_shape=[jax.ShapeDtypeStruct((t, n), dt) for dt in (BF16, F32, BF16, F32, BF16)],
        compiler_params=_cparams("parallel", "parallel"),
        name="qkv_proj",
    )(h, w, w, w)


def _layer_norm_silu(dc, g, b):
    mu = jnp.mean(dc, axis=-1, keepdims=True)
    xc = dc - mu
    var = jnp.mean(xc * xc, axis=-1, keepdims=True)
    return _silu((xc * lax.rsqrt(var + EPS)) * g + b)


HALO = 32
CONV_ROWS = 128


def _conv_prompt_kernel(u_ref, halo_ref, prev_ref, w_ref, b_ref, g_ref, beta_ref, o_ref, buf_ref, dc_ref, win_ref):
    i = pl.program_id(0)
    tt, c = u_ref.shape

    @pl.when(i == 0)
    def _():
        buf_ref[0:HALO, :] = prev_ref[...]

    @pl.when(i > 0)
    def _():
        buf_ref[0:HALO, :] = halo_ref[...]

    buf_ref[HALO:HALO + tt, :] = u_ref[...]
    first = HALO - (CONV_KERNEL - 1)
    by_phase = {}
    for j in range(CONV_KERNEL):
        by_phase.setdefault((first + j) % SUBLANES, []).append(first + j)
    rows = min(tt, CONV_ROWS)

    def lane_chunk(cc, carry):
        ls = pl.ds(pl.multiple_of(cc * LANES, LANES), LANES)
        for r0 in range(0, tt, rows):
            acc = jnp.broadcast_to(b_ref[:, ls], (rows, LANES))
            for offsets in by_phase.values():
                lo, hi = offsets[0], offsets[-1]
                win_ref[0:hi - lo + rows, :] = buf_ref[r0 + lo:r0 + hi + rows, ls]
                for o in offsets:
                    j = o - first
                    acc = acc + w_ref[j:j + 1, ls] * win_ref[o - lo:o - lo + rows, :]
            dc_ref[r0:r0 + rows, ls] = acc
        return carry

    lax.fori_loop(0, c // LANES, lane_chunk, 0)
    o_ref[...] = _layer_norm_silu(dc_ref[...], g_ref[...], beta_ref[...]).astype(o_ref.dtype)


def conv_prompt(u, prev_halo, w_dw, b_dw, ln_g, ln_b, tt=256):
    t, c = u.shape
    ratio = tt // HALO
    return pl.pallas_call(
        _conv_prompt_kernel,
        grid=(t // tt,),
        in_specs=[
            pl.BlockSpec((tt, c), lambda i: (i, 0)),
            pl.BlockSpec((HALO, c), lambda i: (jnp.maximum(i * ratio - 1, 0), 0)),
            pl.BlockSpec((HALO, c), lambda i: (0, 0)),
            pl.BlockSpec((CONV_KERNEL, c), lambda i: (0, 0)),
            pl.BlockSpec((1, c), lambda i: (0, 0)),
            pl.BlockSpec((1, c), lambda i: (0, 0)),
            pl.BlockSpec((1, c), lambda i: (0, 0)),
        ],
        out_specs=pl.BlockSpec((tt, c), lambda i: (i, 0)),
        out_shape=jax.ShapeDtypeStruct((t, c), BF16),
        scratch_shapes=[pltpu.VMEM((HALO + tt, c), F32), pltpu.VMEM((tt, c), F32),
                        pltpu.VMEM((HALO + min(tt, CONV_ROWS), LANES), F32)],
        compiler_params=_cparams("arbitrary"),
        name="conv_prompt",
    )(u, u, prev_halo, w_dw, b_dw.reshape(1, c), ln_g.reshape(1, c), ln_b.reshape(1, c))


def _conv_sample_kernel(u_ref, prev_ref, w_ref, b_ref, g_ref, beta_ref, act_ref, new_ref, dc_ref):
    bb = u_ref.shape[0]
    hist = CONV_KERNEL - 1
    w_hist = w_ref[0:hist, :]
    w_last = w_ref[hist:hist + 1, :]
    for b in range(bb):
        prev = prev_ref[b]
        u_row = u_ref[b:b + 1, :]
        dc_ref[b:b + 1, :] = (jnp.sum(prev * w_hist, axis=0, keepdims=True) + w_last * u_row + b_ref[...])
        new_ref[b, 0:hist - 1, :] = prev_ref[b, 1:hist, :]
        new_ref[b, hist - 1:hist, :] = u_row
    act_ref[...] = _layer_norm_silu(dc_ref[...], g_ref[...], beta_ref[...]).astype(act_ref.dtype)


def conv_sample(u, prev, w_dw, b_dw, ln_g, ln_b, bb=8):
    n, c = u.shape
    hist = prev.shape[1]
    return pl.pallas_call(
        _conv_sample_kernel,
        grid=(n // bb,),
        in_specs=[
            pl.BlockSpec((bb, c), lambda i: (i, 0)),
            pl.BlockSpec((bb, hist, c), lambda i: (i, 0, 0)),
            pl.BlockSpec((CONV_KERNEL, c), lambda i: (0, 0)),
            pl.BlockSpec((1, c), lambda i: (0, 0)),
            pl.BlockSpec((1, c), lambda i: (0, 0)),
            pl.BlockSpec((1, c), lambda i: (0, 0)),
        ],
        out_specs=[
            pl.BlockSpec((bb, c), lambda i: (i, 0)),
            pl.BlockSpec((bb, hist, c), lambda i: (i, 0, 0)),
        ],
        out_shape=[jax.ShapeDtypeStruct((n, c), BF16), jax.ShapeDtypeStruct((n, hist, c), F32)],
        scratch_shapes=[pltpu.VMEM((bb, c), F32)],
        compiler_params=_cparams("parallel"),
        name="conv_sample",
    )(u, prev, w_dw, b_dw.reshape(1, c), ln_g.reshape(1, c), ln_b.reshape(1, c))


def _softplus2(z2):
    sign_bit = jnp.uint32(0x80000000)
    neg_abs = lax.bitcast_convert_type(lax.bitcast_convert_type(z2, jnp.uint32) | sign_bit, F32)
    return jnp.maximum(z2, 0.0) + jnp.log(1.0 + jnp.exp2(neg_abs)) * LOG2E


def _strict_lower(n):
    return (lax.broadcasted_iota(jnp.int32, (n, n), 0) > lax.broadcasted_iota(jnp.int32, (n, n), 1)).astype(BF16)


def _attn_prompt_kernel(bias_ref, q_ref, k_ref, v_ref, o_ref, acc_ref, carry_ref, *, tk, heads):
    qi = pl.program_id(1)
    tq = q_ref.shape[0]
    n_diag = tq // tk
    upper = _strict_lower(tk)
    acc_ref[...] = jnp.zeros_like(acc_ref)
    carry_ref[...] = jnp.zeros_like(carry_ref)

    def tile(hh, kb, masked):
        cols = slice(hh * HEAD_DIM, (hh + 1) * HEAD_DIM)
        start = pl.multiple_of(kb * tk, tk)
        ks = k_ref[pl.ds(start, tk), cols]
        vs = v_ref[pl.ds(start, tk), cols]
        z = _dot_nt(q_ref[:, cols], ks) + bias_ref[hh]
        sp = _softplus2(z)
        ls = z - sp
        if masked:
            t_pos = qi * tq + lax.broadcasted_iota(jnp.int32, (tq, tk), 0)
            s_pos = kb * tk + lax.broadcasted_iota(jnp.int32, (tq, tk), 1)
            valid = s_pos < t_pos
            sp = jnp.where(valid, sp, 0.0)
        spb = sp.astype(BF16)
        later = _dot(spb, upper)
        carry = carry_ref[hh]
        a = jnp.exp2(ls - later - carry)
        if masked:
            a = jnp.where(valid, a, 0.0)
        acc_ref[hh] += _dot(a.astype(BF16), vs)
        carry_ref[hh] = carry + later[:, 0:1] + spb[:, 0:1].astype(F32)

    for d in range(n_diag):
        for hh in range(heads):
            tile(hh, qi * n_diag + (n_diag - 1 - d), True)

    def body(j, c):
        for d in range(n_diag):
            for hh in range(heads):
                tile(hh, (qi - j) * n_diag - 1 - d, False)
        return c

    lax.fori_loop(0, qi, body, 0)
    for hh in range(heads):
        o_ref[:, hh * HEAD_DIM:(hh + 1) * HEAD_DIM] = acc_ref[hh].astype(o_ref.dtype)


def attn_prompt(q, k, v, b_sb, tq=512, tk=MXU_DIM, heads=2):
    t = q.shape[0]
    tq = min(tq, t)
    width = heads * HEAD_DIM
    bias = jnp.broadcast_to((b_sb.astype(F32) * LOG2E)[:, None, None], (N_HEADS, 1, tk))
    kern = functools.partial(_attn_prompt_kernel, tk=tk, heads=heads)
    return pl.pallas_call(
        kern,
        grid=(N_HEADS // heads, t // tq),
        in_specs=[
            pl.BlockSpec((heads, 1, tk), lambda g, i: (g, 0, 0)),
            pl.BlockSpec((tq, width), lambda g, i: (i, g)),
            pl.BlockSpec((t, width), lambda g, i: (0, g)),
            pl.BlockSpec((t, width), lambda g, i: (0, g)),
        ],
        out_specs=pl.BlockSpec((tq, width), lambda g, i: (i, g)),
        out_shape=jax.ShapeDtypeStruct(q.shape, BF16),
        scratch_shapes=[pltpu.VMEM((heads, tq, HEAD_DIM), F32), pltpu.VMEM((heads, tq, 1), F32)],
        compiler_params=_cparams("parallel", "arbitrary"),
        name="attn_prompt",
    )(bias, q, k, v)


def _attn_sample_kernel(pt_ref, bias_ref, q_ref, *refs, pp, page):
    k_refs, v_refs = refs[:pp], refs[pp:2 * pp]
    o_ref, acc_ref, carry_ref = refs[2 * pp:]
    j = pl.program_id(1)
    nh = N_HEADS

    @pl.when(j == 0)
    def _():
        acc_ref[...] = jnp.zeros_like(acc_ref)
        carry_ref[...] = jnp.zeros_like(carry_ref)

    q = q_ref[0]
    head = lax.broadcasted_iota(jnp.int32, (nh, LANES), 0)

    def head_rows(page_refs, h):
        return jnp.concatenate(
            [r[0, 0, pl.ds(h, page, stride=nh), :] for r in page_refs], axis=0).astype(BF16)

    z_pages = [jnp.zeros((nh, page), F32) for _ in range(pp)]
    for h in range(nh):
        z_h = _dot_nt(q, head_rows(k_refs, h))
        z_pages = [jnp.where(head == h, z_h[:, p * page:(p + 1) * page], z_pages[p]) for p in range(pp)]
    z = jnp.concatenate(z_pages, axis=0) + bias_ref[...]
    sp = _softplus2(z)
    spb = sp.astype(BF16)
    later = _dot(spb, _strict_lower(page))
    total = later[:, 0:1] + spb[:, 0:1].astype(F32)
    carry = carry_ref[...]
    carries = [None] * pp
    for p in reversed(range(pp)):
        carries[p] = carry
        carry = carry + total[p * nh:(p + 1) * nh]
    carry_ref[...] = carry
    a = jnp.exp2((z - sp) - later - jnp.concatenate(carries, axis=0))
    a_cat = jnp.concatenate([a[p * nh:(p + 1) * nh] for p in range(pp)], axis=1).astype(BF16)
    acc = acc_ref[...]
    for h in range(nh):
        acc = jnp.where(head == h, acc + _dot(a_cat, head_rows(v_refs, h)), acc)
    acc_ref[...] = acc

    @pl.when(j == pl.num_programs(1) - 1)
    def _():
        o_ref[0] = acc.astype(o_ref.dtype)


def attn_sample(q, cache_k, cache_v, layer, page_table, b_sb, pp=8):
    n_seq, n_pages = page_table.shape
    n_layers, n_pool, page = cache_k.shape[:3]
    pp = min(pp, n_pages)
    steps = n_pages // pp
    ck = cache_k.reshape(n_layers, n_pool, page * N_HEADS, HEAD_DIM)
    cv = cache_v.reshape(n_layers, n_pool, page * N_HEADS, HEAD_DIM)
    bias = jnp.broadcast_to((b_sb.astype(F32) * LOG2E)[None, :, None], (pp, N_HEADS, page)).reshape(pp * N_HEADS, page)

    def page_spec(p):
        def index(b, j, pt):
            return (layer, pt[b * n_pages + (steps - 1 - j) * pp + p], 0, 0)
        return pl.BlockSpec((1, 1, page * N_HEADS, HEAD_DIM), index)

    kern = functools.partial(_attn_sample_kernel, pp=pp, page=page)
    grid_spec = pltpu.PrefetchScalarGridSpec(
        num_scalar_prefetch=1,
        grid=(n_seq, steps),
        in_specs=[
            pl.BlockSpec((pp * N_HEADS, page), lambda b, j, pt: (0, 0)),
            pl.BlockSpec((1, N_HEADS, HEAD_DIM), lambda b, j, pt: (b, 0, 0)),
            *[page_spec(p) for p in range(pp)],
            *[page_spec(p) for p in range(pp)],
        ],
        out_specs=pl.BlockSpec((1, N_HEADS, HEAD_DIM), lambda b, j, pt: (b, 0, 0)),
        scratch_shapes=[pltpu.VMEM((N_HEADS, HEAD_DIM), F32), pltpu.VMEM((N_HEADS, 1), F32)],
    )
    return pl.pallas_call(
        kern,
        grid_spec=grid_spec,
        out_shape=jax.ShapeDtypeStruct((n_seq, N_HEADS, HEAD_DIM), BF16),
        compiler_params=_cparams("parallel", "arbitrary"),
        name="attn_sample",
    )(page_table.reshape(-1), bias, q, *([ck] * pp), *([cv] * pp))


def _merge_kernel(h_ref, act_ref, att_ref, wga_ref, wgb_ref, wco_ref, wao_ref, o_ref):
    h = h_ref[...]
    conv_out = _dot(act_ref[...], wco_ref[...])
    att_out = _dot(att_ref[...], wao_ref[...])
    ga = jax.nn.sigmoid(_dot(h, wga_ref[...]))
    gb = jax.nn.sigmoid(_dot(h, wgb_ref[...]))
    o_ref[...] = (ga * conv_out + gb * att_out).astype(o_ref.dtype)


def merge(h, act, att, w, col_ga, col_gb, wco, wao, tm, tn=512):
    t, d = h.shape
    c = act.shape[1]
    n = wco.shape[1]
    return pl.pallas_call(
        _merge_kernel,
        grid=(t // tm, n // tn),
        in_specs=[
            pl.BlockSpec((tm, d), lambda i, j: (i, 0)),
            pl.BlockSpec((tm, c), lambda i, j: (i, 0)),
            pl.BlockSpec((tm, c), lambda i, j: (i, 0)),
            _cols(d, tn, col_ga),
            _cols(d, tn, col_gb),
            pl.BlockSpec((c, tn), lambda i, j: (0, j)),
            pl.BlockSpec((c, tn), lambda i, j: (0, j)),
        ],
        out_specs=pl.BlockSpec((tm, tn), lambda i, j: (i, j)),
        out_shape=jax.ShapeDtypeStruct((t, n), BF16),
        compiler_params=_cparams("parallel", "parallel"),
        name="merge",
    )(h, act, att, w, w, wco, wao)


def _first_argmax(x, m, lane):
    return jnp.min(jnp.where(x == m, lane, float(LANES)), axis=-1, keepdims=True)


def _router(logits):
    lane = lax.broadcasted_iota(jnp.int32, logits.shape, 1).astype(F32)
    neg = -jnp.inf
    lg = jnp.where(lane >= N_EXPERTS, jnp.where(lane < N_EXPERTS + N_GROUPS, logits, neg), neg)
    mg = jnp.max(lg, axis=-1, keepdims=True)
    p_grp = 1.0 / jnp.sum(jnp.exp(lg - mg), axis=-1, keepdims=True)
    g_idx = _first_argmax(lg, mg, lane) - N_EXPERTS
    in_group = jnp.floor(lane * (1.0 / EXPERTS_PER_GROUP)) == g_idx
    le = jnp.where(in_group, logits, neg)
    m1 = jnp.max(le, axis=-1, keepdims=True)
    i1 = _first_argmax(le, m1, lane)
    le2 = jnp.where(lane == i1, neg, le)
    m2 = jnp.max(le2, axis=-1, keepdims=True)
    i2 = _first_argmax(le2, m2, lane)
    e2 = jnp.exp(m2 - m1)
    w1 = p_grp / (1.0 + e2)
    w2 = w1 * e2
    gate = jnp.where(lane == i1, w1, 0.0) + jnp.where(lane == i2, w2, 0.0)
    route = jnp.where(lane == 0, i1, jnp.where(lane == 1, i2, jnp.where(lane == 2, w1, jnp.where(lane == 3, w2, 0.0))))
    return gate, route


def _split_dot(x_hi, x_lo, w_hi_ref, w_lo_ref):
    w_hi = w_hi_ref[...]
    return _dot(x_hi, w_hi) + (_dot(x_lo, w_hi) + _dot(x_hi, w_lo_ref[...]))


def _oproj_kernel(m_ref, wo_ref, x_ref, gt_ref, g2_ref, sc_ref, sh_ref, wrh_ref, wrl_ref, br_ref,
                  x1_ref, h2_ref, gate_ref, route_ref):
    x1 = x_ref[...] + gt_ref[...] * _dot(m_ref[...], wo_ref[...])
    x1_ref[...] = x1
    h2 = _rms_mod(x1, g2_ref[...], sc_ref[...], sh_ref[...])
    hi = h2.astype(BF16)
    h2_ref[...] = hi
    lo = (h2 - hi.astype(F32)).astype(BF16)
    gate_ref[...], route_ref[...] = _router(_split_dot(hi, lo, wrh_ref, wrl_ref) + br_ref[...])


def _pad_lanes(w):
    return jnp.pad(w, ((0, 0), (0, LANES - w.shape[-1])))


def _hi_lo(w):
    hi = w.astype(BF16)
    return hi, (w - hi.astype(F32)).astype(BF16)


def oproj_router(m, wo, x, gt, g2, sc, sh, w_rg, b_rg, w_re, b_re, tm):
    t, d = x.shape
    wrh, wrl = _hi_lo(_pad_lanes(jnp.concatenate([w_re, w_rg], axis=1)))
    br = _pad_lanes(jnp.concatenate([b_re, b_rg]).reshape(1, -1))
    full = lambda shape: pl.BlockSpec(shape, lambda i: (0, 0))
    return pl.pallas_call(
        _oproj_kernel,
        grid=(t // tm,),
        in_specs=[
            pl.BlockSpec((tm, d), lambda i: (i, 0)),
            full((d, d)),
            pl.BlockSpec((tm, d), lambda i: (i, 0)),
            _row_spec(gt.shape[0], tm, d),
            full((1, d)),
            _row_spec(sc.shape[0], tm, d),
            _row_spec(sh.shape[0], tm, d),
            full((d, LANES)), full((d, LANES)), full((1, LANES)),
        ],
        out_specs=[
            pl.BlockSpec((tm, d), lambda i: (i, 0)),
            pl.BlockSpec((tm, d), lambda i: (i, 0)),
            pl.BlockSpec((tm, LANES), lambda i: (i, 0)),
            pl.BlockSpec((tm, LANES), lambda i: (i, 0)),
        ],
        out_shape=[
            jax.ShapeDtypeStruct((t, d), F32),
            jax.ShapeDtypeStruct((t, d), BF16),
            jax.ShapeDtypeStruct((t, LANES), F32),
            jax.ShapeDtypeStruct((t, LANES), F32),
        ],
        compiler_params=_cparams("parallel"),
        name="oproj_router",
    )(m, wo, x, gt, g2.reshape(1, d), sc, sh, wrh, wrl, br)


def _moe_kernel(h_ref, gate_ref, wg_ref, wu_ref, wd_ref, x_ref, gt_ref, o_ref, wg16_ref, wu16_ref, wd16_ref, acc_ref):
    e = pl.program_id(0)

    @pl.when(e == 0)
    def _():
        acc_ref[...] = jnp.zeros_like(acc_ref)

    wg = wg_ref[0].astype(BF16)
    wu = wu_ref[0].astype(BF16)
    wd = wd_ref[0].astype(BF16)
    wg16_ref[0] = wg
    wu16_ref[0] = wu
    wd16_ref[0] = wd
    h = h_ref[...]
    gate = gate_ref[...]
    lane = lax.broadcasted_iota(jnp.int32, gate.shape, 1)
    ge = jnp.sum(jnp.where(lane == e, gate, 0.0), axis=-1, keepdims=True)
    act = (_silu(_dot(h, wg)) * _dot(h, wu)) * ge
    acc_ref[...] += _dot(act.astype(BF16), wd)

    @pl.when(e == pl.num_programs(0) - 1)
    def _():
        o_ref[...] = x_ref[...] + gt_ref[...] * acc_ref[...]


def moe_dense(h2, gate, w_eg, w_eu, w_ed, x1, gt):
    t, d = x1.shape
    n_e, _, f = w_eg.shape
    whole = lambda width: pl.BlockSpec((t, width), lambda e: (0, 0))
    up = pl.BlockSpec((1, d, f), lambda e: (e, 0, 0))
    down = pl.BlockSpec((1, f, d), lambda e: (e, 0, 0))
    return pl.pallas_call(
        _moe_kernel,
        grid=(n_e,),
        in_specs=[whole(d), whole(LANES), up, up, down, whole(d),
                  pl.BlockSpec((gt.shape[0], d), lambda e: (0, 0))],
        out_specs=[whole(d), up, up, down],
        out_shape=[
            jax.ShapeDtypeStruct((t, d), F32),
            jax.ShapeDtypeStruct(w_eg.shape, BF16),
            jax.ShapeDtypeStruct(w_eu.shape, BF16),
            jax.ShapeDtypeStruct(w_ed.shape, BF16),
        ],
        scratch_shapes=[pltpu.VMEM((t, d), F32)],
        compiler_params=_cparams("arbitrary"),
        name="moe_dense",
    )(h2, gate, w_eg, w_eu, w_ed, x1, gt)


SEG_ALIGN = BF16_ROWS
EXPERT_ROWS = 128
SORT_ROWS = MXU_DIM


ROUTE_ROWS = 8


def _route_pos_kernel(route_ref, col_ref, row_ref, seg_ref):
    route = route_ref[...]
    tm = route.shape[0]
    lane = lax.broadcasted_iota(jnp.int32, route.shape, 1).astype(F32)
    oh1 = lane == route[:, 0:1]
    oh2 = lane == route[:, 1:2]
    both = jnp.where(oh1, 1.0, jnp.where(oh2, 1.0, 0.0))
    before = (lax.broadcasted_iota(jnp.int32, (tm, tm), 1) < lax.broadcasted_iota(jnp.int32, (tm, tm), 0))
    earlier = _dot(jnp.where(before, 1.0, 0.0).astype(BF16), both.astype(BF16))
    units = jnp.floor((jnp.sum(both, axis=0, keepdims=True) + (SEG_ALIGN - 1)) * (1.0 / SEG_ALIGN))
    lower = (lax.broadcasted_iota(jnp.int32, (LANES, LANES), 0) < lax.broadcasted_iota(jnp.int32, (LANES, LANES), 1))
    start = SEG_ALIGN * _dot(jnp.broadcast_to(units, (ROUTE_ROWS, LANES)).astype(BF16),
                             jnp.where(lower, 1.0, 0.0).astype(BF16))[0:1]
    size = SEG_ALIGN * units
    spot = earlier + start
    pos1 = jnp.sum(jnp.where(oh1, spot, 0.0), axis=-1, keepdims=True)
    pos2 = jnp.sum(jnp.where(oh2, spot, 0.0), axis=-1, keepdims=True)
    col = jnp.where(lane == 0, pos1, jnp.where(lane == 1, pos2, route))
    col_ref[...] = col
    row_ref[0] = jnp.transpose(col)[0:ROUTE_ROWS, :]
    sub = lax.broadcasted_iota(jnp.int32, (ROUTE_ROWS, LANES), 0)
    trips = jnp.floor((size + (EXPERT_ROWS - 1)) * (1.0 / EXPERT_ROWS))
    seg_ref[0] = jnp.where(sub == 0, start, jnp.where(sub == 1, start + size, jnp.where(sub == 2, trips, 0.0)))


def route_pos(route, tm):
    t = route.shape[0]
    nt = t // tm
    col, row, seg = pl.pallas_call(
        _route_pos_kernel,
        grid=(nt,),
        in_specs=[pl.BlockSpec((tm, LANES), lambda i: (i, 0))],
        out_specs=[
            pl.BlockSpec((tm, LANES), lambda i: (i, 0)),
            pl.BlockSpec((1, ROUTE_ROWS, tm), lambda i: (i, 0, 0)),
            pl.BlockSpec((1, ROUTE_ROWS, LANES), lambda i: (i, 0, 0)),
        ],
        out_shape=[
            jax.ShapeDtypeStruct((t, LANES), F32),
            jax.ShapeDtypeStruct((nt, ROUTE_ROWS, tm), F32),
            jax.ShapeDtypeStruct((nt, ROUTE_ROWS, LANES), F32),
        ],
        compiler_params=_cparams("parallel"),
        name="route_pos",
    )(route)
    seg = seg[:, 0:3, 0:N_EXPERTS].astype(jnp.int32)
    return col, row, seg[:, 0].reshape(-1), seg[:, 1].reshape(-1), seg[:, 2].reshape(-1)


def _moe_routed_kernel(start_ref, end_ref, trips_ref, h_ref, col_ref, row_ref, wg_ref, wu_ref, wd_ref, o_ref,
                       xy_ref, wrow_ref):
    i = pl.program_id(0)
    e = pl.program_id(1)
    n_e = pl.num_programs(1)
    tm = h_ref.shape[0]
    rows = xy_ref.shape[0]

    @pl.when(e == 0)
    def _():
        pos1, pos2 = row_ref[0, 0:1, :], row_ref[0, 1:2, :]
        w1, w2 = row_ref[0, 2:3, :], row_ref[0, 3:4, :]

        def sort(c, carry):
            r0 = pl.multiple_of(c * SORT_ROWS, SORT_ROWS)
            row = (r0 + lax.broadcasted_iota(jnp.int32, (SORT_ROWS, tm), 0)).astype(F32)
            hit1 = pos1 == row
            hit2 = pos2 == row
            pick = jnp.where(hit1, 1.0, jnp.where(hit2, 1.0, 0.0)).astype(BF16)
            xy_ref[pl.ds(r0, SORT_ROWS), :] = _dot(pick, h_ref[...]).astype(BF16)
            wrow_ref[pl.ds(r0, SORT_ROWS), :] = jnp.sum(
                jnp.where(hit1, w1, jnp.where(hit2, w2, 0.0)), axis=-1, keepdims=True)
            return carry
        lax.fori_loop(0, rows // SORT_ROWS, sort, 0)

    seg = i * n_e + e
    seg_start = start_ref[seg]
    seg_end = end_ref[seg]

    def expert(c, carry):
        r0 = pl.multiple_of(seg_start + c * EXPERT_ROWS, SEG_ALIGN)
        xs = xy_ref[pl.ds(r0, EXPERT_ROWS), :]
        w_row = wrow_ref[pl.ds(r0, EXPERT_ROWS), :]
        act = (_silu(_dot(xs, wg_ref[0])) * _dot(xs, wu_ref[0])) * w_row
        y = _dot(act.astype(BF16), wd_ref[0]).astype(BF16)
        row = r0 + lax.broadcasted_iota(jnp.int32, (EXPERT_ROWS, 1), 0)
        xy_ref[pl.ds(r0, EXPERT_ROWS), :] = jnp.where(row < seg_end, y, xs)
        return carry

    lax.fori_loop(0, trips_ref[seg], expert, 0)

    @pl.when(e == n_e - 1)
    def _():
        def combine(c, carry):
            t0 = pl.multiple_of(c * SORT_ROWS, SORT_ROWS)
            row = lax.broadcasted_iota(jnp.int32, (SORT_ROWS, rows), 1).astype(F32)
            hit1 = col_ref[pl.ds(t0, SORT_ROWS), 0:1] == row
            hit2 = col_ref[pl.ds(t0, SORT_ROWS), 1:2] == row
            pick = jnp.where(hit1, 1.0, jnp.where(hit2, 1.0, 0.0)).astype(BF16)
            o_ref[pl.ds(t0, SORT_ROWS), :] = _dot(pick, xy_ref[...])
            return carry
        lax.fori_loop(0, tm // SORT_ROWS, combine, 0)


def moe_routed(h2, route, w_eg, w_eu, w_ed, tm):
    t, d = h2.shape
    n_e, _, f = w_eg.shape
    rows = _round_up(2 * tm + n_e * (SEG_ALIGN - 1) + EXPERT_ROWS, SORT_ROWS)
    col, row, start, end, trips = route_pos(route, tm)
    grid_spec = pltpu.PrefetchScalarGridSpec(
        num_scalar_prefetch=3,
        grid=(t // tm, n_e),
        in_specs=[
            pl.BlockSpec((tm, d), lambda i, e, *_: (i, 0)),
            pl.BlockSpec((tm, LANES), lambda i, e, *_: (i, 0)),
            pl.BlockSpec((1, ROUTE_ROWS, tm), lambda i, e, *_: (i, 0, 0)),
            pl.BlockSpec((1, d, f), lambda i, e, *_: (e, 0, 0)),
            pl.BlockSpec((1, d, f), lambda i, e, *_: (e, 0, 0)),
            pl.BlockSpec((1, f, d), lambda i, e, *_: (e, 0, 0)),
        ],
        out_specs=pl.BlockSpec((tm, d), lambda i, e, *_: (i, 0)),
        scratch_shapes=[pltpu.VMEM((rows, d), BF16), pltpu.VMEM((rows, 1), F32)],
    )
    return pl.pallas_call(
        _moe_routed_kernel,
        grid_spec=grid_spec,
        out_shape=jax.ShapeDtypeStruct((t, d), F32),
        compiler_params=_cparams("parallel", "arbitrary", vmem=VMEM_LIMIT_MOE),
        name="moe_routed",
    )(start, end, trips, h2, col, row, w_eg, w_eu, w_ed)


def _mix_rows(x, h, mods, wts, tm, conv_fn, attn_fn):
    _, _, gt1, sh2, sc2, _ = mods
    d = x.shape[1]
    w_in = wts["w_in"]
    c_ch = wts["w_co"].shape[0]
    attn_w = wts["w_ao"].shape[0]
    col_q = 2 * c_ch
    col_g = col_q + 3 * attn_w
    u = glu_proj(h, w_in, 0, c_ch, c_ch, tm)
    q, k32, k16, v32, v16 = qkv_proj(h, w_in, col_q, attn_w, tm)
    act, new_conv = conv_fn(u)
    att = attn_fn(q, k16, v16)
    m = merge(h, act, att, w_in, col_g, col_g + d, wts["w_co"], wts["w_ao"], tm)
    x1, h2, gate, route = oproj_router(m, wts["w_o"], x, gt1, wts["g_n2"], sc2, sh2,
                                       wts["w_rg"], wts["b_rg"], wts["w_re"], wts["b_re"], min(tm, 256))
    return x1, h2, gate, route, k32, v32, new_conv


def kernel(x_prompt, x_sample, cache_k, cache_v, state_conv, page_table, c_prompt, c_sample, w_ada, b_ada, g_n1, w_in, b_sb, w_dw, b_dw, ln_g, ln_b, w_conv_out, w_attn_out, w_o, g_n2, w_rg, b_rg, w_re, b_re, w_eg, w_eu, w_ed, g_final):
    batch, seq, d = x_prompt.shape
    n_seq, t_new, _ = x_sample.shape
    assert batch == 1 and t_new == 1, "one prompt sequence and one new token per sample sequence"
    n_layers = w_in.shape[0]
    c_ch = w_dw.shape[-1]
    attn_w = N_HEADS * HEAD_DIM
    hist = CONV_KERNEL - 1
    assert seq >= hist

    mod_rows = n_seq + BF16_ROWS
    c_all = jnp.zeros((mod_rows, d), F32).at[:n_seq].set(c_sample).at[n_seq:n_seq + 1].set(c_prompt)
    mod = ada_mod(c_all, w_ada, b_ada)

    tm_p = min(1024, seq)
    tm_n = min(512, seq)
    xp = x_prompt.reshape(seq, d)
    xs = x_sample.reshape(n_seq, d)
    pending = None
    outs = {name: [] for name in ("kp", "vp", "cp", "ks", "vs", "cs")}
    zero_halo = jnp.zeros((HALO, c_ch), F32)
    zeros_row = jnp.zeros((1, d), F32)
    for l in range(n_layers):
        wts = dict(
            g_n2=g_n2[l], w_in=w_in[l].astype(BF16),
            w_co=w_conv_out[l].astype(BF16), w_ao=w_attn_out[l].astype(BF16), w_o=w_o[l].astype(BF16),
            w_rg=w_rg[l], b_rg=b_rg[l], w_re=w_re[l], b_re=b_re[l],
        )
        mods_s = tuple(mod[l, :n_seq, i * d:(i + 1) * d] for i in range(6))
        mods_p = tuple(mod[l, n_seq:n_seq + 1, i * d:(i + 1) * d] for i in range(6))

        def conv_p(u, l=l):
            act = conv_prompt(u, zero_halo, w_dw[l], b_dw[l], ln_g[l], ln_b[l])
            return act, u[seq - hist:]

        def attn_p(q, k16, v16, l=l):
            return attn_prompt(q, k16, v16, b_sb[l])

        def conv_s(u, l=l):
            return conv_sample(u, state_conv[l], w_dw[l], b_dw[l], ln_g[l], ln_b[l])

        def attn_s(q, k16, v16, l=l):
            att = attn_sample(q.reshape(n_seq, N_HEADS, HEAD_DIM), cache_k, cache_v, l, page_table, b_sb[l])
            return att.reshape(n_seq, attn_w)

        hs = norm_mod(xs, g_n1[l], mods_s[1], mods_s[0], BF16, n_seq)
        x1s, h2s, gate_s, _, ks, vs, cs = _mix_rows(xs, hs, mods_s, wts, n_seq, conv_s, attn_s)
        xs, weg, weu, wed = moe_dense(h2s, gate_s, w_eg[l], w_eu[l], w_ed[l], x1s, mods_s[5])

        if pending is None:
            hp = norm_mod(xp, g_n1[l], mods_p[1], mods_p[0], BF16, tm_p)
        else:
            xp, hp = resid_norm(*pending, g_n1[l], mods_p[1], mods_p[0], BF16, tm_n)
        x1, h2, _, route, kp, vp, cp = _mix_rows(xp, hp, mods_p, wts, tm_p, conv_p, attn_p)
        pending = (x1, moe_routed(h2, route, weg, weu, wed, tm_p), mods_p[5])

        outs["kp"].append(kp.reshape(1, seq, N_HEADS, HEAD_DIM))
        outs["vp"].append(vp.reshape(1, seq, N_HEADS, HEAD_DIM))
        outs["cp"].append(cp.reshape(1, hist, c_ch))
        outs["ks"].append(ks.reshape(n_seq, 1, N_HEADS, HEAD_DIM))
        outs["vs"].append(vs.reshape(n_seq, 1, N_HEADS, HEAD_DIM))
        outs["cs"].append(cs)

    _, y_prompt = resid_norm(*pending, g_final, zeros_row, zeros_row, F32, tm_n)
    y_sample = norm_mod(xs, g_final, zeros_row, zeros_row, F32, n_seq)
    return (y_prompt.reshape(1, seq, d), y_sample.reshape(n_seq, 1, d), jnp.stack(outs["kp"]), jnp.stack(outs["vp"]),
            jnp.stack(outs["cp"]), jnp.stack(outs["ks"]), jnp.stack(outs["vs"]), jnp.stack(outs["cs"]))
```

```python
import functools

import jax
import jax.numpy as jnp
from jax import lax
from jax.experimental import pallas as pl
from jax.experimental.pallas import tpu as pltpu

F32 = jnp.float32
BF16 = jnp.bfloat16

N_HEADS = 8
HEAD_DIM = 128
CONV_KERNEL = 31
N_GROUPS = 4
EXPERTS_PER_GROUP = 8
N_EXPERTS = N_GROUPS * EXPERTS_PER_GROUP
EPS = 1e-6

LOG2E = 1.4426950408889634
Q_SCALE = HEAD_DIM ** -0.5 * LOG2E

LANES = 128
SUBLANES = 8
BF16_ROWS = 16
MXU_DIM = 256
V7X_VMEM_BYTES = 64 * 1024 * 1024
VMEM_LIMIT = V7X_VMEM_BYTES - 8 * 1024 * 1024
VMEM_LIMIT_MOE = V7X_VMEM_BYTES - 4 * 1024 * 1024


def _cparams(*sem, vmem=VMEM_LIMIT):
    return pltpu.CompilerParams(dimension_semantics=sem, vmem_limit_bytes=vmem)


def _dot(a, b):
    return jnp.dot(a, b, preferred_element_type=F32)


def _dot_nt(a, b):
    return lax.dot_general(a, b, (((1,), (1,)), ((), ())), preferred_element_type=F32)


def _silu(x):
    return x * jax.nn.sigmoid(x)


def _round_up(x, m):
    return (x + m - 1) // m * m


def _row_spec(rows, tm, width):
    if rows == 1:
        return pl.BlockSpec((1, width), lambda i, *_: (0, 0))
    return pl.BlockSpec((tm, width), lambda i, *_: (i, 0))


def _ada_kernel(c_ref, w_ref, b_ref, o_ref):
    a = _silu(c_ref[...]).astype(BF16)
    o_ref[0] = _dot(a, w_ref[0].astype(BF16)) + b_ref[0]


def ada_mod(c_all, w_ada, b_ada, tn=512):
    n_layers, d, n = w_ada.shape
    rows = c_all.shape[0]
    return pl.pallas_call(
        _ada_kernel,
        grid=(n_layers, n // tn),
        in_specs=[
            pl.BlockSpec((rows, d), lambda l, j: (0, 0)),
            pl.BlockSpec((1, d, tn), lambda l, j: (l, 0, j)),
            pl.BlockSpec((1, 1, tn), lambda l, j: (l, 0, j)),
        ],
        out_specs=pl.BlockSpec((1, rows, tn), lambda l, j: (l, 0, j)),
        out_shape=jax.ShapeDtypeStruct((n_layers, rows, n), F32),
        compiler_params=_cparams("parallel", "parallel"),
        name="ada_mod",
    )(c_all, w_ada, b_ada.reshape(n_layers, 1, n))


def _rms_mod(x, g, sc, sh):
    y = x * lax.rsqrt(jnp.mean(x * x, axis=-1, keepdims=True) + EPS)
    return (y * g) * (1.0 + sc) + sh


def _norm_mod_kernel(x_ref, g_ref, sc_ref, sh_ref, o_ref):
    o_ref[...] = _rms_mod(x_ref[...], g_ref[...], sc_ref[...], sh_ref[...]).astype(o_ref.dtype)


def norm_mod(x, g, sc, sh, out_dtype, tm):
    t, d = x.shape
    return pl.pallas_call(
        _norm_mod_kernel,
        grid=(t // tm,),
        in_specs=[
            pl.BlockSpec((tm, d), lambda i: (i, 0)),
            pl.BlockSpec((1, d), lambda i: (0, 0)),
            _row_spec(sc.shape[0], tm, d),
            _row_spec(sh.shape[0], tm, d),
        ],
        out_specs=pl.BlockSpec((tm, d), lambda i: (i, 0)),
        out_shape=jax.ShapeDtypeStruct((t, d), out_dtype),
        compiler_params=_cparams("parallel"),
        name="norm_mod",
    )(x, g.reshape(1, d), sc, sh)


def _resid_norm_kernel(x_ref, m_ref, gt_ref, g_ref, sc_ref, sh_ref, x2_ref, o_ref):
    x = x_ref[...] + gt_ref[...] * m_ref[...]
    x2_ref[...] = x
    o_ref[...] = _rms_mod(x, g_ref[...], sc_ref[...], sh_ref[...]).astype(o_ref.dtype)


def resid_norm(x, m, gt, g, sc, sh, out_dtype, tm):
    t, d = x.shape
    return pl.pallas_call(
        _resid_norm_kernel,
        grid=(t // tm,),
        in_specs=[
            pl.BlockSpec((tm, d), lambda i: (i, 0)),
            pl.BlockSpec((tm, d), lambda i: (i, 0)),
            _row_spec(gt.shape[0], tm, d),
            pl.BlockSpec((1, d), lambda i: (0, 0)),
            _row_spec(sc.shape[0], tm, d),
            _row_spec(sh.shape[0], tm, d),
        ],
        out_specs=[pl.BlockSpec((tm, d), lambda i: (i, 0)), pl.BlockSpec((tm, d), lambda i: (i, 0))],
        out_shape=[jax.ShapeDtypeStruct((t, d), F32), jax.ShapeDtypeStruct((t, d), out_dtype)],
        compiler_params=_cparams("parallel"),
        name="resid_norm",
    )(x, m, gt, g.reshape(1, d), sc, sh)


def _glu_kernel(h_ref, wa_ref, wb_ref, u_ref):
    h = h_ref[...]
    u_ref[...] = _dot(h, wa_ref[...]) * jax.nn.sigmoid(_dot(h, wb_ref[...]))


def _cols(d, tn, layer, col0):
    first = col0 // tn
    assert first * tn == col0
    return pl.BlockSpec((None, d, tn), lambda i, j: (layer, 0, first + j))


def glu_proj(h, w, layer, col_a, col_b, n, tm, tn=512):
    t, d = h.shape
    return pl.pallas_call(
        _glu_kernel,
        grid=(t // tm, n // tn),
        in_specs=[pl.BlockSpec((tm, d), lambda i, j: (i, 0)),
                  _cols(d, tn, layer, col_a), _cols(d, tn, layer, col_b)],
        out_specs=pl.BlockSpec((tm, tn), lambda i, j: (i, j)),
        out_shape=jax.ShapeDtypeStruct((t, n), F32),
        compiler_params=_cparams("parallel", "parallel"),
        name="glu_proj",
    )(h, w, w)


def _qkv_kernel(h_ref, wq_ref, wk_ref, wv_ref, k_all_ref, v_all_ref, q_ref, k32_ref, k16_ref, v32_ref, v16_ref):
    del k_all_ref, v_all_ref
    h = h_ref[...]
    q_ref[...] = (_dot(h, wq_ref[...]) * Q_SCALE).astype(q_ref.dtype)
    k = _dot(h, wk_ref[...])
    k32_ref[...] = k
    k16_ref[...] = k.astype(k16_ref.dtype)
    v = _dot(h, wv_ref[...])
    v32_ref[...] = v
    v16_ref[...] = v.astype(v16_ref.dtype)


def qkv_proj(h, w, layer, col_q, n, k_all, v_all, tm, tn=512):
    t, d = h.shape
    out = pl.BlockSpec((tm, tn), lambda i, j: (i, j))
    out_l = pl.BlockSpec((None, tm, tn), lambda i, j: (layer, i, j))
    stacked = jax.ShapeDtypeStruct(k_all.shape, F32)
    flat = lambda dt: jax.ShapeDtypeStruct((t, n), dt)
    return pl.pallas_call(
        _qkv_kernel,
        grid=(t // tm, n // tn),
        in_specs=[pl.BlockSpec((tm, d), lambda i, j: (i, 0)),
                  _cols(d, tn, layer, col_q), _cols(d, tn, layer, col_q + n), _cols(d, tn, layer, col_q + 2 * n),
                  pl.BlockSpec(memory_space=pl.ANY), pl.BlockSpec(memory_space=pl.ANY)],
        out_specs=[out, out_l, out, out_l, out],
        out_shape=[flat(BF16), stacked, flat(BF16), stacked, flat(BF16)],
        input_output_aliases={4: 1, 5: 3},
        compiler_params=_cparams("parallel", "parallel"),
        name="qkv_proj",
    )(h, w, w, w, k_all, v_all)


def _layer_norm_silu(dc, g, b):
    mu = jnp.mean(dc, axis=-1, keepdims=True)
    xc = dc - mu
    var = jnp.mean(xc * xc, axis=-1, keepdims=True)
    return _silu((xc * lax.rsqrt(var + EPS)) * g + b)


HALO = 32
CONV_ROWS = 128


def _conv_prompt_kernel(u_ref, halo_ref, prev_ref, w_ref, b_ref, g_ref, beta_ref, o_ref, buf_ref, dc_ref, win_ref):
    i = pl.program_id(0)
    tt, c = u_ref.shape

    @pl.when(i == 0)
    def _():
        buf_ref[0:HALO, :] = prev_ref[...]

    @pl.when(i > 0)
    def _():
        buf_ref[0:HALO, :] = halo_ref[...]

    buf_ref[HALO:HALO + tt, :] = u_ref[...]
    first = HALO - (CONV_KERNEL - 1)
    by_phase = {}
    for j in range(CONV_KERNEL):
        by_phase.setdefault((first + j) % SUBLANES, []).append(first + j)
    rows = min(tt, CONV_ROWS)

    def lane_chunk(cc, carry):
        ls = pl.ds(pl.multiple_of(cc * LANES, LANES), LANES)
        for r0 in range(0, tt, rows):
            acc = jnp.broadcast_to(b_ref[:, ls], (rows, LANES))
            for offsets in by_phase.values():
                lo, hi = offsets[0], offsets[-1]
                win_ref[0:hi - lo + rows, :] = buf_ref[r0 + lo:r0 + hi + rows, ls]
                for o in offsets:
                    j = o - first
                    acc = acc + w_ref[j:j + 1, ls] * win_ref[o - lo:o - lo + rows, :]
            dc_ref[r0:r0 + rows, ls] = acc
        return carry

    lax.fori_loop(0, c // LANES, lane_chunk, 0)
    o_ref[...] = _layer_norm_silu(dc_ref[...], g_ref[...], beta_ref[...]).astype(o_ref.dtype)


def conv_prompt(u, prev_halo, w_dw, b_dw, ln_g, ln_b, tt=256):
    t, c = u.shape
    ratio = tt // HALO
    return pl.pallas_call(
        _conv_prompt_kernel,
        grid=(t // tt,),
        in_specs=[
            pl.BlockSpec((tt, c), lambda i: (i, 0)),
            pl.BlockSpec((HALO, c), lambda i: (jnp.maximum(i * ratio - 1, 0), 0)),
            pl.BlockSpec((HALO, c), lambda i: (0, 0)),
            pl.BlockSpec((CONV_KERNEL, c), lambda i: (0, 0)),
            pl.BlockSpec((1, c), lambda i: (0, 0)),
            pl.BlockSpec((1, c), lambda i: (0, 0)),
            pl.BlockSpec((1, c), lambda i: (0, 0)),
        ],
        out_specs=pl.BlockSpec((tt, c), lambda i: (i, 0)),
        out_shape=jax.ShapeDtypeStruct((t, c), BF16),
        scratch_shapes=[pltpu.VMEM((HALO + tt, c), F32), pltpu.VMEM((tt, c), F32),
                        pltpu.VMEM((HALO + min(tt, CONV_ROWS), LANES), F32)],
        compiler_params=_cparams("arbitrary"),
        name="conv_prompt",
    )(u, u, prev_halo, w_dw, b_dw.reshape(1, c), ln_g.reshape(1, c), ln_b.reshape(1, c))


def _conv_sample_kernel(u_ref, prev_ref, w_ref, b_ref, g_ref, beta_ref, new_all_ref, act_ref, new_ref, dc_ref):
    del new_all_ref
    bb = u_ref.shape[0]
    hist = CONV_KERNEL - 1
    w_hist = w_ref[0:hist, :]
    w_last = w_ref[hist:hist + 1, :]
    for b in range(bb):
        prev = prev_ref[b]
        u_row = u_ref[b:b + 1, :]
        dc_ref[b:b + 1, :] = (jnp.sum(prev * w_hist, axis=0, keepdims=True) + w_last * u_row + b_ref[...])
        new_ref[b, 0:hist - 1, :] = prev_ref[b, 1:hist, :]
        new_ref[b, hist - 1:hist, :] = u_row
    act_ref[...] = _layer_norm_silu(dc_ref[...], g_ref[...], beta_ref[...]).astype(act_ref.dtype)


def conv_sample(u, state, layer, w_dw, b_dw, ln_g, ln_b, new_all, bb=8):
    n, c = u.shape
    hist = state.shape[2]
    hist_spec = pl.BlockSpec((None, bb, hist, c), lambda i: (layer, i, 0, 0))
    return pl.pallas_call(
        _conv_sample_kernel,
        grid=(n // bb,),
        in_specs=[
            pl.BlockSpec((bb, c), lambda i: (i, 0)),
            hist_spec,
            pl.BlockSpec((CONV_KERNEL, c), lambda i: (0, 0)),
            pl.BlockSpec((1, c), lambda i: (0, 0)),
            pl.BlockSpec((1, c), lambda i: (0, 0)),
            pl.BlockSpec((1, c), lambda i: (0, 0)),
            pl.BlockSpec(memory_space=pl.ANY),
        ],
        out_specs=[pl.BlockSpec((bb, c), lambda i: (i, 0)), hist_spec],
        out_shape=[jax.ShapeDtypeStruct((n, c), BF16), jax.ShapeDtypeStruct(new_all.shape, F32)],
        input_output_aliases={6: 1},
        scratch_shapes=[pltpu.VMEM((bb, c), F32)],
        compiler_params=_cparams("parallel"),
        name="conv_sample",
    )(u, state, w_dw, b_dw.reshape(1, c), ln_g.reshape(1, c), ln_b.reshape(1, c), new_all)


def _softplus2(z2):
    sign_bit = jnp.uint32(0x80000000)
    neg_abs = lax.bitcast_convert_type(lax.bitcast_convert_type(z2, jnp.uint32) | sign_bit, F32)
    return jnp.maximum(z2, 0.0) + jnp.log(1.0 + jnp.exp2(neg_abs)) * LOG2E


def _strict_lower(n):
    return (lax.broadcasted_iota(jnp.int32, (n, n), 0) > lax.broadcasted_iota(jnp.int32, (n, n), 1)).astype(BF16)


def _attn_prompt_kernel(bias_ref, q_ref, k_ref, v_ref, o_ref, acc_ref, carry_ref, *, tk, heads):
    qi = pl.program_id(1)
    tq = q_ref.shape[0]
    n_diag = tq // tk
    upper = _strict_lower(tk)
    acc_ref[...] = jnp.zeros_like(acc_ref)
    carry_ref[...] = jnp.zeros_like(carry_ref)

    def tile(hh, kb, masked):
        cols = slice(hh * HEAD_DIM, (hh + 1) * HEAD_DIM)
        start = pl.multiple_of(kb * tk, tk)
        ks = k_ref[pl.ds(start, tk), cols]
        vs = v_ref[pl.ds(start, tk), cols]
        z = _dot_nt(q_ref[:, cols], ks) + bias_ref[hh]
        sp = _softplus2(z)
        ls = z - sp
        if masked:
            t_pos = qi * tq + lax.broadcasted_iota(jnp.int32, (tq, tk), 0)
            s_pos = kb * tk + lax.broadcasted_iota(jnp.int32, (tq, tk), 1)
            valid = s_pos < t_pos
            sp = jnp.where(valid, sp, 0.0)
        spb = sp.astype(BF16)
        later = _dot(spb, upper)
        carry = carry_ref[hh]
        a = jnp.exp2(ls - later - carry)
        if masked:
            a = jnp.where(valid, a, 0.0)
        acc_ref[hh] += _dot(a.astype(BF16), vs)
        carry_ref[hh] = carry + later[:, 0:1] + spb[:, 0:1].astype(F32)

    for d in range(n_diag):
        for hh in range(heads):
            tile(hh, qi * n_diag + (n_diag - 1 - d), True)

    def body(j, c):
        for d in range(n_diag):
            for hh in range(heads):
                tile(hh, (qi - j) * n_diag - 1 - d, False)
        return c

    lax.fori_loop(0, qi, body, 0)
    for hh in range(heads):
        o_ref[:, hh * HEAD_DIM:(hh + 1) * HEAD_DIM] = acc_ref[hh].astype(o_ref.dtype)


def attn_prompt(q, k, v, b_sb, tq=512, tk=MXU_DIM, heads=2):
    t = q.shape[0]
    tq = min(tq, t)
    width = heads * HEAD_DIM
    bias = jnp.broadcast_to((b_sb.astype(F32) * LOG2E)[:, None, None], (N_HEADS, 1, tk))
    kern = functools.partial(_attn_prompt_kernel, tk=tk, heads=heads)
    return pl.pallas_call(
        kern,
        grid=(N_HEADS // heads, t // tq),
        in_specs=[
            pl.BlockSpec((heads, 1, tk), lambda g, i: (g, 0, 0)),
            pl.BlockSpec((tq, width), lambda g, i: (i, g)),
            pl.BlockSpec((t, width), lambda g, i: (0, g)),
            pl.BlockSpec((t, width), lambda g, i: (0, g)),
        ],
        out_specs=pl.BlockSpec((tq, width), lambda g, i: (i, g)),
        out_shape=jax.ShapeDtypeStruct(q.shape, BF16),
        scratch_shapes=[pltpu.VMEM((heads, tq, HEAD_DIM), F32), pltpu.VMEM((heads, tq, 1), F32)],
        compiler_params=_cparams("parallel", "arbitrary"),
        name="attn_prompt",
    )(bias, q, k, v)


def _attn_sample_kernel(pt_ref, bias_ref, q_ref, *refs, pp, page):
    k_refs, v_refs = refs[:pp], refs[pp:2 * pp]
    o_ref, acc_ref, carry_ref = refs[2 * pp:]
    j = pl.program_id(1)
    nh = N_HEADS

    @pl.when(j == 0)
    def _():
        acc_ref[...] = jnp.zeros_like(acc_ref)
        carry_ref[...] = jnp.zeros_like(carry_ref)

    q = q_ref[0]
    head = lax.broadcasted_iota(jnp.int32, (nh, LANES), 0)

    def head_rows(page_refs, h):
        return jnp.concatenate(
            [r[0, 0, pl.ds(h, page, stride=nh), :] for r in page_refs], axis=0).astype(BF16)

    z_pages = [jnp.zeros((nh, page), F32) for _ in range(pp)]
    for h in range(nh):
        z_h = _dot_nt(q, head_rows(k_refs, h))
        z_pages = [jnp.where(head == h, z_h[:, p * page:(p + 1) * page], z_pages[p]) for p in range(pp)]
    z = jnp.concatenate(z_pages, axis=0) + bias_ref[...]
    sp = _softplus2(z)
    spb = sp.astype(BF16)
    later = _dot(spb, _strict_lower(page))
    total = later[:, 0:1] + spb[:, 0:1].astype(F32)
    carry = carry_ref[...]
    carries = [None] * pp
    for p in reversed(range(pp)):
        carries[p] = carry
        carry = carry + total[p * nh:(p + 1) * nh]
    carry_ref[...] = carry
    a = jnp.exp2((z - sp) - later - jnp.concatenate(carries, axis=0))
    a_cat = jnp.concatenate([a[p * nh:(p + 1) * nh] for p in range(pp)], axis=1).astype(BF16)
    acc = acc_ref[...]
    for h in range(nh):
        acc = jnp.where(head == h, acc + _dot(a_cat, head_rows(v_refs, h)), acc)
    acc_ref[...] = acc

    @pl.when(j == pl.num_programs(1) - 1)
    def _():
        o_ref[0] = acc.astype(o_ref.dtype)


def attn_sample(q, cache_k, cache_v, layer, page_table, b_sb, pp=16):
    n_seq, n_pages = page_table.shape
    n_layers, n_pool, page = cache_k.shape[:3]
    pp = min(pp, n_pages)
    steps = n_pages // pp
    ck = cache_k.reshape(n_layers, n_pool, page * N_HEADS, HEAD_DIM)
    cv = cache_v.reshape(n_layers, n_pool, page * N_HEADS, HEAD_DIM)
    bias = jnp.broadcast_to((b_sb.astype(F32) * LOG2E)[None, :, None], (pp, N_HEADS, page)).reshape(pp * N_HEADS, page)

    def page_spec(p):
        def index(b, j, pt):
            return (layer, pt[b * n_pages + (steps - 1 - j) * pp + p], 0, 0)
        return pl.BlockSpec((1, 1, page * N_HEADS, HEAD_DIM), index)

    kern = functools.partial(_attn_sample_kernel, pp=pp, page=page)
    grid_spec = pltpu.PrefetchScalarGridSpec(
        num_scalar_prefetch=1,
        grid=(n_seq, steps),
        in_specs=[
            pl.BlockSpec((pp * N_HEADS, page), lambda b, j, pt: (0, 0)),
            pl.BlockSpec((1, N_HEADS, HEAD_DIM), lambda b, j, pt: (b, 0, 0)),
            *[page_spec(p) for p in range(pp)],
            *[page_spec(p) for p in range(pp)],
        ],
        out_specs=pl.BlockSpec((1, N_HEADS, HEAD_DIM), lambda b, j, pt: (b, 0, 0)),
        scratch_shapes=[pltpu.VMEM((N_HEADS, HEAD_DIM), F32), pltpu.VMEM((N_HEADS, 1), F32)],
    )
    return pl.pallas_call(
        kern,
        grid_spec=grid_spec,
        out_shape=jax.ShapeDtypeStruct((n_seq, N_HEADS, HEAD_DIM), BF16),
        compiler_params=_cparams("parallel", "arbitrary"),
        name="attn_sample",
    )(page_table.reshape(-1), bias, q, *([ck] * pp), *([cv] * pp))


def _merge_kernel(h_ref, act_ref, att_ref, wga_ref, wgb_ref, wco_ref, wao_ref, o_ref):
    h = h_ref[...]
    conv_out = _dot(act_ref[...], wco_ref[...])
    att_out = _dot(att_ref[...], wao_ref[...])
    ga = jax.nn.sigmoid(_dot(h, wga_ref[...]))
    gb = jax.nn.sigmoid(_dot(h, wgb_ref[...]))
    o_ref[...] = (ga * conv_out + gb * att_out).astype(o_ref.dtype)


def merge(h, act, att, w, layer, col_ga, col_gb, wco, wao, tm, tn=512):
    t, d = h.shape
    c = act.shape[1]
    n = wco.shape[1]
    return pl.pallas_call(
        _merge_kernel,
        grid=(t // tm, n // tn),
        in_specs=[
            pl.BlockSpec((tm, d), lambda i, j: (i, 0)),
            pl.BlockSpec((tm, c), lambda i, j: (i, 0)),
            pl.BlockSpec((tm, c), lambda i, j: (i, 0)),
            _cols(d, tn, layer, col_ga),
            _cols(d, tn, layer, col_gb),
            pl.BlockSpec((c, tn), lambda i, j: (0, j)),
            pl.BlockSpec((c, tn), lambda i, j: (0, j)),
        ],
        out_specs=pl.BlockSpec((tm, tn), lambda i, j: (i, j)),
        out_shape=jax.ShapeDtypeStruct((t, n), BF16),
        compiler_params=_cparams("parallel", "parallel"),
        name="merge",
    )(h, act, att, w, w, wco, wao)


def _first_argmax(x, m, lane):
    return jnp.min(jnp.where(x == m, lane, float(LANES)), axis=-1, keepdims=True)


def _router(logits):
    lane = lax.broadcasted_iota(jnp.int32, logits.shape, 1).astype(F32)
    neg = -jnp.inf
    lg = jnp.where(lane >= N_EXPERTS, jnp.where(lane < N_EXPERTS + N_GROUPS, logits, neg), neg)
    mg = jnp.max(lg, axis=-1, keepdims=True)
    p_grp = 1.0 / jnp.sum(jnp.exp(lg - mg), axis=-1, keepdims=True)
    g_idx = _first_argmax(lg, mg, lane) - N_EXPERTS
    in_group = jnp.floor(lane * (1.0 / EXPERTS_PER_GROUP)) == g_idx
    le = jnp.where(in_group, logits, neg)
    m1 = jnp.max(le, axis=-1, keepdims=True)
    i1 = _first_argmax(le, m1, lane)
    le2 = jnp.where(lane == i1, neg, le)
    m2 = jnp.max(le2, axis=-1, keepdims=True)
    i2 = _first_argmax(le2, m2, lane)
    e2 = jnp.exp(m2 - m1)
    w1 = p_grp / (1.0 + e2)
    w2 = w1 * e2
    gate = jnp.where(lane == i1, w1, 0.0) + jnp.where(lane == i2, w2, 0.0)
    route = jnp.where(lane == 0, i1, jnp.where(lane == 1, i2, jnp.where(lane == 2, w1, jnp.where(lane == 3, w2, 0.0))))
    return gate, route


def _split_dot(x_hi, x_lo, w_hi_ref, w_lo_ref):
    w_hi = w_hi_ref[...]
    return _dot(x_hi, w_hi) + (_dot(x_lo, w_hi) + _dot(x_hi, w_lo_ref[...]))


def _oproj_kernel(m_ref, wo_ref, x_ref, gt_ref, g2_ref, sc_ref, sh_ref, wrh_ref, wrl_ref, br_ref,
                  x1_ref, h2_ref, gate_ref, route_ref):
    x1 = x_ref[...] + gt_ref[...] * _dot(m_ref[...], wo_ref[...])
    x1_ref[...] = x1
    h2 = _rms_mod(x1, g2_ref[...], sc_ref[...], sh_ref[...])
    hi = h2.astype(BF16)
    h2_ref[...] = hi
    lo = (h2 - hi.astype(F32)).astype(BF16)
    gate_ref[...], route_ref[...] = _router(_split_dot(hi, lo, wrh_ref, wrl_ref) + br_ref[...])


def _pad_lanes(w):
    return jnp.pad(w, ((0, 0), (0, LANES - w.shape[-1])))


def _hi_lo(w):
    hi = w.astype(BF16)
    return hi, (w - hi.astype(F32)).astype(BF16)


def oproj_router(m, wo, x, gt, g2, sc, sh, w_rg, b_rg, w_re, b_re, tm):
    t, d = x.shape
    wrh, wrl = _hi_lo(_pad_lanes(jnp.concatenate([w_re, w_rg], axis=1)))
    br = _pad_lanes(jnp.concatenate([b_re, b_rg]).reshape(1, -1))
    full = lambda shape: pl.BlockSpec(shape, lambda i: (0, 0))
    return pl.pallas_call(
        _oproj_kernel,
        grid=(t // tm,),
        in_specs=[
            pl.BlockSpec((tm, d), lambda i: (i, 0)),
            full((d, d)),
            pl.BlockSpec((tm, d), lambda i: (i, 0)),
            _row_spec(gt.shape[0], tm, d),
            full((1, d)),
            _row_spec(sc.shape[0], tm, d),
            _row_spec(sh.shape[0], tm, d),
            full((d, LANES)), full((d, LANES)), full((1, LANES)),
        ],
        out_specs=[
            pl.BlockSpec((tm, d), lambda i: (i, 0)),
            pl.BlockSpec((tm, d), lambda i: (i, 0)),
            pl.BlockSpec((tm, LANES), lambda i: (i, 0)),
            pl.BlockSpec((tm, LANES), lambda i: (i, 0)),
        ],
        out_shape=[
            jax.ShapeDtypeStruct((t, d), F32),
            jax.ShapeDtypeStruct((t, d), BF16),
            jax.ShapeDtypeStruct((t, LANES), F32),
            jax.ShapeDtypeStruct((t, LANES), F32),
        ],
        compiler_params=_cparams("parallel"),
        name="oproj_router",
    )(m, wo, x, gt, g2.reshape(1, d), sc, sh, wrh, wrl, br)


def _moe_kernel(h_ref, gate_ref, wg_ref, wu_ref, wd_ref, x_ref, gt_ref, o_ref, wg16_ref, wu16_ref, wd16_ref, acc_ref):
    e = pl.program_id(0)

    @pl.when(e == 0)
    def _():
        acc_ref[...] = jnp.zeros_like(acc_ref)

    wg = wg_ref[0].astype(BF16)
    wu = wu_ref[0].astype(BF16)
    wd = wd_ref[0].astype(BF16)
    wg16_ref[0] = wg
    wu16_ref[0] = wu
    wd16_ref[0] = wd
    h = h_ref[...]
    gate = gate_ref[...]
    lane = lax.broadcasted_iota(jnp.int32, gate.shape, 1)
    ge = jnp.sum(jnp.where(lane == e, gate, 0.0), axis=-1, keepdims=True)
    act = (_silu(_dot(h, wg)) * _dot(h, wu)) * ge
    acc_ref[...] += _dot(act.astype(BF16), wd)

    @pl.when(e == pl.num_programs(0) - 1)
    def _():
        o_ref[...] = x_ref[...] + gt_ref[...] * acc_ref[...]


def moe_dense(h2, gate, w_eg, w_eu, w_ed, layer, x1, gt):
    t, d = x1.shape
    _, n_e, _, f = w_eg.shape
    whole = lambda width: pl.BlockSpec((t, width), lambda e: (0, 0))
    up_in = pl.BlockSpec((None, 1, d, f), lambda e: (layer, e, 0, 0))
    down_in = pl.BlockSpec((None, 1, f, d), lambda e: (layer, e, 0, 0))
    up = pl.BlockSpec((1, d, f), lambda e: (e, 0, 0))
    down = pl.BlockSpec((1, f, d), lambda e: (e, 0, 0))
    return pl.pallas_call(
        _moe_kernel,
        grid=(n_e,),
        in_specs=[whole(d), whole(LANES), up_in, up_in, down_in, whole(d),
                  pl.BlockSpec((gt.shape[0], d), lambda e: (0, 0))],
        out_specs=[whole(d), up, up, down],
        out_shape=[
            jax.ShapeDtypeStruct((t, d), F32),
            jax.ShapeDtypeStruct((n_e, d, f), BF16),
            jax.ShapeDtypeStruct((n_e, d, f), BF16),
            jax.ShapeDtypeStruct((n_e, f, d), BF16),
        ],
        scratch_shapes=[pltpu.VMEM((t, d), F32)],
        compiler_params=_cparams("arbitrary"),
        name="moe_dense",
    )(h2, gate, w_eg, w_eu, w_ed, x1, gt)


SEG_ALIGN = BF16_ROWS
EXPERT_ROWS = 128
SORT_ROWS = MXU_DIM


ROUTE_ROWS = 8


def _route_pos_kernel(route_ref, col_ref, row_ref, seg_ref):
    route = route_ref[...]
    tm = route.shape[0]
    lane = lax.broadcasted_iota(jnp.int32, route.shape, 1).astype(F32)
    oh1 = lane == route[:, 0:1]
    oh2 = lane == route[:, 1:2]
    both = jnp.where(oh1, 1.0, jnp.where(oh2, 1.0, 0.0))
    before = (lax.broadcasted_iota(jnp.int32, (tm, tm), 1) < lax.broadcasted_iota(jnp.int32, (tm, tm), 0))
    earlier = _dot(jnp.where(before, 1.0, 0.0).astype(BF16), both.astype(BF16))
    units = jnp.floor((jnp.sum(both, axis=0, keepdims=True) + (SEG_ALIGN - 1)) * (1.0 / SEG_ALIGN))
    lower = (lax.broadcasted_iota(jnp.int32, (LANES, LANES), 0) < lax.broadcasted_iota(jnp.int32, (LANES, LANES), 1))
    start = SEG_ALIGN * _dot(jnp.broadcast_to(units, (ROUTE_ROWS, LANES)).astype(BF16),
                             jnp.where(lower, 1.0, 0.0).astype(BF16))[0:1]
    size = SEG_ALIGN * units
    spot = earlier + start
    pos1 = jnp.sum(jnp.where(oh1, spot, 0.0), axis=-1, keepdims=True)
    pos2 = jnp.sum(jnp.where(oh2, spot, 0.0), axis=-1, keepdims=True)
    col = jnp.where(lane == 0, pos1, jnp.where(lane == 1, pos2, route))
    col_ref[...] = col
    row_ref[0] = jnp.transpose(col)[0:ROUTE_ROWS, :]
    sub = lax.broadcasted_iota(jnp.int32, (ROUTE_ROWS, LANES), 0)
    trips = jnp.floor((size + (EXPERT_ROWS - 1)) * (1.0 / EXPERT_ROWS))
    seg_ref[0] = jnp.where(sub == 0, start, jnp.where(sub == 1, start + size, jnp.where(sub == 2, trips, 0.0)))


def route_pos(route, tm):
    t = route.shape[0]
    nt = t // tm
    col, row, seg = pl.pallas_call(
        _route_pos_kernel,
        grid=(nt,),
        in_specs=[pl.BlockSpec((tm, LANES), lambda i: (i, 0))],
        out_specs=[
            pl.BlockSpec((tm, LANES), lambda i: (i, 0)),
            pl.BlockSpec((1, ROUTE_ROWS, tm), lambda i: (i, 0, 0)),
            pl.BlockSpec((1, ROUTE_ROWS, LANES), lambda i: (i, 0, 0)),
        ],
        out_shape=[
            jax.ShapeDtypeStruct((t, LANES), F32),
            jax.ShapeDtypeStruct((nt, ROUTE_ROWS, tm), F32),
            jax.ShapeDtypeStruct((nt, ROUTE_ROWS, LANES), F32),
        ],
        compiler_params=_cparams("parallel"),
        name="route_pos",
    )(route)
    seg = seg[:, 0:3, 0:N_EXPERTS].astype(jnp.int32)
    return col, row, seg[:, 0].reshape(-1), seg[:, 1].reshape(-1), seg[:, 2].reshape(-1)


def _moe_routed_kernel(start_ref, end_ref, trips_ref, h_ref, col_ref, row_ref, wg_ref, wu_ref, wd_ref, o_ref,
                       xy_ref, wrow_ref):
    i = pl.program_id(0)
    e = pl.program_id(1)
    n_e = pl.num_programs(1)
    tm = h_ref.shape[0]
    rows = xy_ref.shape[0]

    @pl.when(e == 0)
    def _():
        pos1, pos2 = row_ref[0, 0:1, :], row_ref[0, 1:2, :]
        w1, w2 = row_ref[0, 2:3, :], row_ref[0, 3:4, :]

        def sort(c, carry):
            r0 = pl.multiple_of(c * SORT_ROWS, SORT_ROWS)
            row = (r0 + lax.broadcasted_iota(jnp.int32, (SORT_ROWS, tm), 0)).astype(F32)
            hit1 = pos1 == row
            hit2 = pos2 == row
            pick = jnp.where(hit1, 1.0, jnp.where(hit2, 1.0, 0.0)).astype(BF16)
            xy_ref[pl.ds(r0, SORT_ROWS), :] = _dot(pick, h_ref[...]).astype(BF16)
            wrow_ref[pl.ds(r0, SORT_ROWS), :] = jnp.sum(
                jnp.where(hit1, w1, jnp.where(hit2, w2, 0.0)), axis=-1, keepdims=True)
            return carry
        lax.fori_loop(0, rows // SORT_ROWS, sort, 0)

    seg = i * n_e + e
    seg_start = start_ref[seg]
    seg_end = end_ref[seg]

    def expert(c, carry):
        r0 = pl.multiple_of(seg_start + c * EXPERT_ROWS, SEG_ALIGN)
        xs = xy_ref[pl.ds(r0, EXPERT_ROWS), :]
        w_row = wrow_ref[pl.ds(r0, EXPERT_ROWS), :]
        act = (_silu(_dot(xs, wg_ref[0])) * _dot(xs, wu_ref[0])) * w_row
        y = _dot(act.astype(BF16), wd_ref[0]).astype(BF16)
        row = r0 + lax.broadcasted_iota(jnp.int32, (EXPERT_ROWS, 1), 0)
        xy_ref[pl.ds(r0, EXPERT_ROWS), :] = jnp.where(row < seg_end, y, xs)
        return carry

    lax.fori_loop(0, trips_ref[seg], expert, 0)

    @pl.when(e == n_e - 1)
    def _():
        def combine(c, carry):
            t0 = pl.multiple_of(c * SORT_ROWS, SORT_ROWS)
            row = lax.broadcasted_iota(jnp.int32, (SORT_ROWS, rows), 1).astype(F32)
            hit1 = col_ref[pl.ds(t0, SORT_ROWS), 0:1] == row
            hit2 = col_ref[pl.ds(t0, SORT_ROWS), 1:2] == row
            pick = jnp.where(hit1, 1.0, jnp.where(hit2, 1.0, 0.0)).astype(BF16)
            o_ref[pl.ds(t0, SORT_ROWS), :] = _dot(pick, xy_ref[...])
            return carry
        lax.fori_loop(0, tm // SORT_ROWS, combine, 0)


def moe_routed(h2, route, w_eg, w_eu, w_ed, tm):
    t, d = h2.shape
    n_e, _, f = w_eg.shape
    rows = _round_up(2 * tm + n_e * (SEG_ALIGN - 1) + EXPERT_ROWS, SORT_ROWS)
    col, row, start, end, trips = route_pos(route, tm)
    grid_spec = pltpu.PrefetchScalarGridSpec(
        num_scalar_prefetch=3,
        grid=(t // tm, n_e),
        in_specs=[
            pl.BlockSpec((tm, d), lambda i, e, *_: (i, 0)),
            pl.BlockSpec((tm, LANES), lambda i, e, *_: (i, 0)),
            pl.BlockSpec((1, ROUTE_ROWS, tm), lambda i, e, *_: (i, 0, 0)),
            pl.BlockSpec((1, d, f), lambda i, e, *_: (e, 0, 0)),
            pl.BlockSpec((1, d, f), lambda i, e, *_: (e, 0, 0)),
            pl.BlockSpec((1, f, d), lambda i, e, *_: (e, 0, 0)),
        ],
        out_specs=pl.BlockSpec((tm, d), lambda i, e, *_: (i, 0)),
        scratch_shapes=[pltpu.VMEM((rows, d), BF16), pltpu.VMEM((rows, 1), F32)],
    )
    return pl.pallas_call(
        _moe_routed_kernel,
        grid_spec=grid_spec,
        out_shape=jax.ShapeDtypeStruct((t, d), F32),
        compiler_params=_cparams("parallel", "arbitrary", vmem=VMEM_LIMIT_MOE),
        name="moe_routed",
    )(start, end, trips, h2, col, row, w_eg, w_eu, w_ed)


def _mix_rows(x, h, mods, wts, layer, tm, k_all, v_all, conv_fn, attn_fn):
    _, _, gt1, sh2, sc2, _ = mods
    d = x.shape[1]
    w_in = wts["w_in"]
    c_ch = wts["w_co"].shape[0]
    attn_w = wts["w_ao"].shape[0]
    col_q = 2 * c_ch
    col_g = col_q + 3 * attn_w
    u = glu_proj(h, w_in, layer, 0, c_ch, c_ch, tm)
    q, k_all, k16, v_all, v16 = qkv_proj(h, w_in, layer, col_q, attn_w, k_all, v_all, tm)
    act, new_conv = conv_fn(u)
    att = attn_fn(q, k16, v16)
    m = merge(h, act, att, w_in, layer, col_g, col_g + d, wts["w_co"], wts["w_ao"], tm)
    x1, h2, gate, route = oproj_router(m, wts["w_o"], x, gt1, wts["g_n2"], sc2, sh2,
                                       wts["w_rg"], wts["b_rg"], wts["w_re"], wts["b_re"], min(tm, 256))
    return x1, h2, gate, route, k_all, v_all, new_conv


def kernel(x_prompt, x_sample, cache_k, cache_v, state_conv, page_table, c_prompt, c_sample, w_ada, b_ada, g_n1, w_in, b_sb, w_dw, b_dw, ln_g, ln_b, w_conv_out, w_attn_out, w_o, g_n2, w_rg, b_rg, w_re, b_re, w_eg, w_eu, w_ed, g_final):
    batch, seq, d = x_prompt.shape
    n_seq, t_new, _ = x_sample.shape
    assert batch == 1 and t_new == 1, "one prompt sequence and one new token per sample sequence"
    n_layers = w_in.shape[0]
    c_ch = w_dw.shape[-1]
    attn_w = N_HEADS * HEAD_DIM
    hist = CONV_KERNEL - 1
    assert seq >= hist

    mod_rows = n_seq + BF16_ROWS
    c_all = jnp.zeros((mod_rows, d), F32).at[:n_seq].set(c_sample).at[n_seq:n_seq + 1].set(c_prompt)
    mod = ada_mod(c_all, w_ada, b_ada)

    tm_p = min(1024, seq)
    tm_n = min(512, seq)
    xp = x_prompt.reshape(seq, d)
    xs = x_sample.reshape(n_seq, d)
    pending = None
    zero_halo = jnp.zeros((HALO, c_ch), F32)
    zeros_row = jnp.zeros((1, d), F32)
    kp = jnp.zeros((n_layers, seq, attn_w), F32)
    vp = jnp.zeros((n_layers, seq, attn_w), F32)
    ks = jnp.zeros((n_layers, n_seq, attn_w), F32)
    vs = jnp.zeros((n_layers, n_seq, attn_w), F32)
    cs = jnp.zeros(state_conv.shape, F32)
    cp = []
    w_in16 = w_in.astype(BF16)
    for l in range(n_layers):
        wts = dict(
            g_n2=g_n2[l], w_in=w_in16,
            w_co=w_conv_out[l].astype(BF16), w_ao=w_attn_out[l].astype(BF16), w_o=w_o[l].astype(BF16),
            w_rg=w_rg[l], b_rg=b_rg[l], w_re=w_re[l], b_re=b_re[l],
        )
        mods_s = tuple(mod[l, :n_seq, i * d:(i + 1) * d] for i in range(6))
        mods_p = tuple(mod[l, n_seq:n_seq + 1, i * d:(i + 1) * d] for i in range(6))

        def conv_p(u, l=l):
            act = conv_prompt(u, zero_halo, w_dw[l], b_dw[l], ln_g[l], ln_b[l])
            return act, u[seq - hist:]

        def attn_p(q, k16, v16, l=l):
            return attn_prompt(q, k16, v16, b_sb[l])

        def conv_s(u, l=l, cs=cs):
            return conv_sample(u, state_conv, l, w_dw[l], b_dw[l], ln_g[l], ln_b[l], cs)

        def attn_s(q, k16, v16, l=l):
            att = attn_sample(q.reshape(n_seq, N_HEADS, HEAD_DIM), cache_k, cache_v, l, page_table, b_sb[l])
            return att.reshape(n_seq, attn_w)

        hs = norm_mod(xs, g_n1[l], mods_s[1], mods_s[0], BF16, n_seq)
        x1s, h2s, gate_s, _, ks, vs, cs = _mix_rows(xs, hs, mods_s, wts, l, n_seq, ks, vs, conv_s, attn_s)
        xs, weg, weu, wed = moe_dense(h2s, gate_s, w_eg, w_eu, w_ed, l, x1s, mods_s[5])

        if pending is None:
            hp = norm_mod(xp, g_n1[l], mods_p[1], mods_p[0], BF16, tm_p)
        else:
            xp, hp = resid_norm(*pending, g_n1[l], mods_p[1], mods_p[0], BF16, tm_n)
        x1, h2, _, route, kp, vp, cp_l = _mix_rows(xp, hp, mods_p, wts, l, tm_p, kp, vp, conv_p, attn_p)
        pending = (x1, moe_routed(h2, route, weg, weu, wed, tm_p), mods_p[5])
        cp.append(cp_l.reshape(1, hist, c_ch))

    _, y_prompt = resid_norm(*pending, g_final, zeros_row, zeros_row, F32, tm_n)
    y_sample = norm_mod(xs, g_final, zeros_row, zeros_row, F32, n_seq)
    return (y_prompt.reshape(1, seq, d), y_sample.reshape(n_seq, 1, d),
            kp.reshape(n_layers, 1, seq, N_HEADS, HEAD_DIM), vp.reshape(n_layers, 1, seq, N_HEADS, HEAD_DIM),
            jnp.stack(cp),
            ks.reshape(n_layers, n_seq, 1, N_HEADS, HEAD_DIM), vs.reshape(n_layers, n_seq, 1, N_HEADS, HEAD_DIM), cs)
```

```python
import functools

import jax
import jax.numpy as jnp
from jax import lax
from jax.experimental import pallas as pl
from jax.experimental.pallas import tpu as pltpu

F32 = jnp.float32
BF16 = jnp.bfloat16

N_HEADS = 8
HEAD_DIM = 128
CONV_KERNEL = 31
N_GROUPS = 4
EXPERTS_PER_GROUP = 8
N_EXPERTS = N_GROUPS * EXPERTS_PER_GROUP
EPS = 1e-6

LOG2E = 1.4426950408889634
Q_SCALE = HEAD_DIM ** -0.5 * LOG2E

LANES = 128
SUBLANES = 8
BF16_ROWS = 16
MXU_DIM = 256
V7X_VMEM_BYTES = 64 * 1024 * 1024
VMEM_LIMIT = V7X_VMEM_BYTES - 8 * 1024 * 1024
VMEM_LIMIT_MOE = V7X_VMEM_BYTES - 4 * 1024 * 1024


def _cparams(*sem, vmem=VMEM_LIMIT):
    return pltpu.CompilerParams(dimension_semantics=sem, vmem_limit_bytes=vmem)


def _dot(a, b):
    return jnp.dot(a, b, preferred_element_type=F32)


def _dot_nt(a, b):
    return lax.dot_general(a, b, (((1,), (1,)), ((), ())), preferred_element_type=F32)


def _silu(x):
    return x * jax.nn.sigmoid(x)


def _round_up(x, m):
    return (x + m - 1) // m * m


def _row_spec(rows, tm, width):
    if rows == 1:
        return pl.BlockSpec((1, width), lambda i, *_: (0, 0))
    return pl.BlockSpec((tm, width), lambda i, *_: (i, 0))


def _ada_kernel(c_ref, w_ref, b_ref, o_ref):
    a = _silu(c_ref[...]).astype(BF16)
    o_ref[0] = _dot(a, w_ref[0].astype(BF16)) + b_ref[0]


def ada_mod(c_all, w_ada, b_ada, tn=512):
    n_layers, d, n = w_ada.shape
    rows = c_all.shape[0]
    return pl.pallas_call(
        _ada_kernel,
        grid=(n_layers, n // tn),
        in_specs=[
            pl.BlockSpec((rows, d), lambda l, j: (0, 0)),
            pl.BlockSpec((1, d, tn), lambda l, j: (l, 0, j)),
            pl.BlockSpec((1, 1, tn), lambda l, j: (l, 0, j)),
        ],
        out_specs=pl.BlockSpec((1, rows, tn), lambda l, j: (l, 0, j)),
        out_shape=jax.ShapeDtypeStruct((n_layers, rows, n), F32),
        compiler_params=_cparams("parallel", "parallel"),
        name="ada_mod",
    )(c_all, w_ada, b_ada.reshape(n_layers, 1, n))


def _rms_mod(x, g, sc, sh):
    y = x * lax.rsqrt(jnp.mean(x * x, axis=-1, keepdims=True) + EPS)
    return (y * g) * (1.0 + sc) + sh


def _norm_mod_kernel(x_ref, g_ref, sc_ref, sh_ref, o_ref):
    o_ref[...] = _rms_mod(x_ref[...], g_ref[...], sc_ref[...], sh_ref[...]).astype(o_ref.dtype)


def norm_mod(x, g, sc, sh, out_dtype, tm):
    t, d = x.shape
    return pl.pallas_call(
        _norm_mod_kernel,
        grid=(t // tm,),
        in_specs=[
            pl.BlockSpec((tm, d), lambda i: (i, 0)),
            pl.BlockSpec((1, d), lambda i: (0, 0)),
            _row_spec(sc.shape[0], tm, d),
            _row_spec(sh.shape[0], tm, d),
        ],
        out_specs=pl.BlockSpec((tm, d), lambda i: (i, 0)),
        out_shape=jax.ShapeDtypeStruct((t, d), out_dtype),
        compiler_params=_cparams("parallel"),
        name="norm_mod",
    )(x, g.reshape(1, d), sc, sh)


def _resid_norm_kernel(x_ref, m_ref, gt_ref, g_ref, sc_ref, sh_ref, x2_ref, o_ref):
    x = x_ref[...] + gt_ref[...] * m_ref[...]
    x2_ref[...] = x
    o_ref[...] = _rms_mod(x, g_ref[...], sc_ref[...], sh_ref[...]).astype(o_ref.dtype)


def resid_norm(x, m, gt, g, sc, sh, out_dtype, tm):
    t, d = x.shape
    return pl.pallas_call(
        _resid_norm_kernel,
        grid=(t // tm,),
        in_specs=[
            pl.BlockSpec((tm, d), lambda i: (i, 0)),
            pl.BlockSpec((tm, d), lambda i: (i, 0)),
            _row_spec(gt.shape[0], tm, d),
            pl.BlockSpec((1, d), lambda i: (0, 0)),
            _row_spec(sc.shape[0], tm, d),
            _row_spec(sh.shape[0], tm, d),
        ],
        out_specs=[pl.BlockSpec((tm, d), lambda i: (i, 0)), pl.BlockSpec((tm, d), lambda i: (i, 0))],
        out_shape=[jax.ShapeDtypeStruct((t, d), F32), jax.ShapeDtypeStruct((t, d), out_dtype)],
        compiler_params=_cparams("parallel"),
        name="resid_norm",
    )(x, m, gt, g.reshape(1, d), sc, sh)


def _glu_kernel(h_ref, wa_ref, wb_ref, u_ref):
    h = h_ref[...]
    u_ref[...] = _dot(h, wa_ref[...]) * jax.nn.sigmoid(_dot(h, wb_ref[...]))


def _cols(d, tn, layer, col0):
    first = col0 // tn
    assert first * tn == col0
    return pl.BlockSpec((None, d, tn), lambda i, j: (layer, 0, first + j))


def glu_proj(h, w, layer, col_a, col_b, n, tm, tn=512):
    t, d = h.shape
    return pl.pallas_call(
        _glu_kernel,
        grid=(t // tm, n // tn),
        in_specs=[pl.BlockSpec((tm, d), lambda i, j: (i, 0)),
                  _cols(d, tn, layer, col_a), _cols(d, tn, layer, col_b)],
        out_specs=pl.BlockSpec((tm, tn), lambda i, j: (i, j)),
        out_shape=jax.ShapeDtypeStruct((t, n), F32),
        compiler_params=_cparams("parallel", "parallel"),
        name="glu_proj",
    )(h, w, w)


def _qkv_kernel(h_ref, wq_ref, wk_ref, wv_ref, k_all_ref, v_all_ref, q_ref, k32_ref, k16_ref, v32_ref, v16_ref):
    del k_all_ref, v_all_ref
    h = h_ref[...]
    q_ref[...] = (_dot(h, wq_ref[...]) * Q_SCALE).astype(q_ref.dtype)
    k = _dot(h, wk_ref[...])
    k32_ref[...] = k
    k16_ref[...] = k.astype(k16_ref.dtype)
    v = _dot(h, wv_ref[...])
    v32_ref[...] = v
    v16_ref[...] = v.astype(v16_ref.dtype)


def qkv_proj(h, w, layer, col_q, n, k_all, v_all, tm, tn=512):
    t, d = h.shape
    out = pl.BlockSpec((tm, tn), lambda i, j: (i, j))
    out_l = pl.BlockSpec((None, tm, tn), lambda i, j: (layer, i, j))
    stacked = jax.ShapeDtypeStruct(k_all.shape, F32)
    flat = lambda dt: jax.ShapeDtypeStruct((t, n), dt)
    return pl.pallas_call(
        _qkv_kernel,
        grid=(t // tm, n // tn),
        in_specs=[pl.BlockSpec((tm, d), lambda i, j: (i, 0)),
                  _cols(d, tn, layer, col_q), _cols(d, tn, layer, col_q + n), _cols(d, tn, layer, col_q + 2 * n),
                  pl.BlockSpec(memory_space=pl.ANY), pl.BlockSpec(memory_space=pl.ANY)],
        out_specs=[out, out_l, out, out_l, out],
        out_shape=[flat(BF16), stacked, flat(BF16), stacked, flat(BF16)],
        input_output_aliases={4: 1, 5: 3},
        compiler_params=_cparams("parallel", "parallel"),
        name="qkv_proj",
    )(h, w, w, w, k_all, v_all)


def _layer_norm_silu(dc, g, b):
    mu = jnp.mean(dc, axis=-1, keepdims=True)
    xc = dc - mu
    var = jnp.mean(xc * xc, axis=-1, keepdims=True)
    return _silu((xc * lax.rsqrt(var + EPS)) * g + b)


HALO = 32
CONV_ROWS = 128


def _conv_prompt_kernel(u_ref, halo_ref, prev_ref, w_ref, b_ref, g_ref, beta_ref, o_ref, buf_ref, dc_ref, win_ref):
    i = pl.program_id(0)
    tt, c = u_ref.shape

    @pl.when(i == 0)
    def _():
        buf_ref[0:HALO, :] = prev_ref[...]

    @pl.when(i > 0)
    def _():
        buf_ref[0:HALO, :] = halo_ref[...]

    buf_ref[HALO:HALO + tt, :] = u_ref[...]
    first = HALO - (CONV_KERNEL - 1)
    by_phase = {}
    for j in range(CONV_KERNEL):
        by_phase.setdefault((first + j) % SUBLANES, []).append(first + j)
    rows = min(tt, CONV_ROWS)

    def lane_chunk(cc, carry):
        ls = pl.ds(pl.multiple_of(cc * LANES, LANES), LANES)
        for r0 in range(0, tt, rows):
            acc = jnp.broadcast_to(b_ref[:, ls], (rows, LANES))
            for offsets in by_phase.values():
                lo, hi = offsets[0], offsets[-1]
                win_ref[0:hi - lo + rows, :] = buf_ref[r0 + lo:r0 + hi + rows, ls]
                for o in offsets:
                    j = o - first
                    acc = acc + w_ref[j:j + 1, ls] * win_ref[o - lo:o - lo + rows, :]
            dc_ref[r0:r0 + rows, ls] = acc
        return carry

    lax.fori_loop(0, c // LANES, lane_chunk, 0)
    o_ref[...] = _layer_norm_silu(dc_ref[...], g_ref[...], beta_ref[...]).astype(o_ref.dtype)


def conv_prompt(u, prev_halo, w_dw, b_dw, ln_g, ln_b, tt=256):
    t, c = u.shape
    ratio = tt // HALO
    return pl.pallas_call(
        _conv_prompt_kernel,
        grid=(t // tt,),
        in_specs=[
            pl.BlockSpec((tt, c), lambda i: (i, 0)),
            pl.BlockSpec((HALO, c), lambda i: (jnp.maximum(i * ratio - 1, 0), 0)),
            pl.BlockSpec((HALO, c), lambda i: (0, 0)),
            pl.BlockSpec((CONV_KERNEL, c), lambda i: (0, 0)),
            pl.BlockSpec((1, c), lambda i: (0, 0)),
            pl.BlockSpec((1, c), lambda i: (0, 0)),
            pl.BlockSpec((1, c), lambda i: (0, 0)),
        ],
        out_specs=pl.BlockSpec((tt, c), lambda i: (i, 0)),
        out_shape=jax.ShapeDtypeStruct((t, c), BF16),
        scratch_shapes=[pltpu.VMEM((HALO + tt, c), F32), pltpu.VMEM((tt, c), F32),
                        pltpu.VMEM((HALO + min(tt, CONV_ROWS), LANES), F32)],
        compiler_params=_cparams("arbitrary"),
        name="conv_prompt",
    )(u, u, prev_halo, w_dw, b_dw.reshape(1, c), ln_g.reshape(1, c), ln_b.reshape(1, c))


def _conv_sample_kernel(u_ref, prev_ref, w_ref, b_ref, g_ref, beta_ref, new_all_ref, act_ref, new_ref, dc_ref):
    del new_all_ref
    bb = u_ref.shape[0]
    hist = CONV_KERNEL - 1
    w_hist = w_ref[0:hist, :]
    w_last = w_ref[hist:hist + 1, :]
    for b in range(bb):
        prev = prev_ref[b]
        u_row = u_ref[b:b + 1, :]
        dc_ref[b:b + 1, :] = (jnp.sum(prev * w_hist, axis=0, keepdims=True) + w_last * u_row + b_ref[...])
        new_ref[b, 0:hist - 1, :] = prev_ref[b, 1:hist, :]
        new_ref[b, hist - 1:hist, :] = u_row
    act_ref[...] = _layer_norm_silu(dc_ref[...], g_ref[...], beta_ref[...]).astype(act_ref.dtype)


def conv_sample(u, state, layer, w_dw, b_dw, ln_g, ln_b, new_all, bb=8):
    n, c = u.shape
    hist = state.shape[2]
    hist_spec = pl.BlockSpec((None, bb, hist, c), lambda i: (layer, i, 0, 0))
    return pl.pallas_call(
        _conv_sample_kernel,
        grid=(n // bb,),
        in_specs=[
            pl.BlockSpec((bb, c), lambda i: (i, 0)),
            hist_spec,
            pl.BlockSpec((CONV_KERNEL, c), lambda i: (0, 0)),
            pl.BlockSpec((1, c), lambda i: (0, 0)),
            pl.BlockSpec((1, c), lambda i: (0, 0)),
            pl.BlockSpec((1, c), lambda i: (0, 0)),
            pl.BlockSpec(memory_space=pl.ANY),
        ],
        out_specs=[pl.BlockSpec((bb, c), lambda i: (i, 0)), hist_spec],
        out_shape=[jax.ShapeDtypeStruct((n, c), BF16), jax.ShapeDtypeStruct(new_all.shape, F32)],
        input_output_aliases={6: 1},
        scratch_shapes=[pltpu.VMEM((bb, c), F32)],
        compiler_params=_cparams("parallel"),
        name="conv_sample",
    )(u, state, w_dw, b_dw.reshape(1, c), ln_g.reshape(1, c), ln_b.reshape(1, c), new_all)


SOFTPLUS_CAP = 64.0


def _softplus2(z2):
    return jnp.maximum(z2, jnp.log(1.0 + jnp.exp2(jnp.minimum(z2, SOFTPLUS_CAP))) * LOG2E)


def _strict_lower(n):
    return (lax.broadcasted_iota(jnp.int32, (n, n), 0) > lax.broadcasted_iota(jnp.int32, (n, n), 1)).astype(BF16)


def _attn_prompt_kernel(bias_ref, q_ref, k_ref, v_ref, o_ref, acc_ref, carry_ref, *, tk, heads):
    qi = pl.program_id(1)
    tq = q_ref.shape[0]
    n_diag = tq // tk
    upper = _strict_lower(tk)
    acc_ref[...] = jnp.zeros_like(acc_ref)
    carry_ref[...] = jnp.zeros_like(carry_ref)

    def tile(hh, kb, masked):
        cols = slice(hh * HEAD_DIM, (hh + 1) * HEAD_DIM)
        start = pl.multiple_of(kb * tk, tk)
        ks = k_ref[pl.ds(start, tk), cols]
        vs = v_ref[pl.ds(start, tk), cols]
        z = _dot_nt(q_ref[:, cols], ks) + bias_ref[hh]
        sp = _softplus2(z)
        ls = z - sp
        if masked:
            t_pos = qi * tq + lax.broadcasted_iota(jnp.int32, (tq, tk), 0)
            s_pos = kb * tk + lax.broadcasted_iota(jnp.int32, (tq, tk), 1)
            valid = s_pos < t_pos
            sp = jnp.where(valid, sp, 0.0)
        spb = sp.astype(BF16)
        later = _dot(spb, upper)
        carry = carry_ref[hh]
        a = jnp.exp2(ls - later)
        if masked:
            a = jnp.where(valid, a, 0.0)
        acc_ref[hh] += _dot(a.astype(BF16), vs) * jnp.exp2(-carry)
        carry_ref[hh] = carry + later[:, 0:1] + spb[:, 0:1].astype(F32)

    for d in range(n_diag):
        for hh in range(heads):
            tile(hh, qi * n_diag + (n_diag - 1 - d), True)

    def body(j, c):
        for d in range(n_diag):
            for hh in range(heads):
                tile(hh, (qi - j) * n_diag - 1 - d, False)
        return c

    lax.fori_loop(0, qi, body, 0)
    for hh in range(heads):
        o_ref[:, hh * HEAD_DIM:(hh + 1) * HEAD_DIM] = acc_ref[hh].astype(o_ref.dtype)


def attn_prompt(q, k, v, b_sb, tq=512, tk=MXU_DIM, heads=2):
    t = q.shape[0]
    tq = min(tq, t)
    width = heads * HEAD_DIM
    bias = jnp.broadcast_to((b_sb.astype(F32) * LOG2E)[:, None, None], (N_HEADS, 1, tk))
    kern = functools.partial(_attn_prompt_kernel, tk=tk, heads=heads)
    return pl.pallas_call(
        kern,
        grid=(N_HEADS // heads, t // tq),
        in_specs=[
            pl.BlockSpec((heads, 1, tk), lambda g, i: (g, 0, 0)),
            pl.BlockSpec((tq, width), lambda g, i: (i, g)),
            pl.BlockSpec((t, width), lambda g, i: (0, g)),
            pl.BlockSpec((t, width), lambda g, i: (0, g)),
        ],
        out_specs=pl.BlockSpec((tq, width), lambda g, i: (i, g)),
        out_shape=jax.ShapeDtypeStruct(q.shape, BF16),
        scratch_shapes=[pltpu.VMEM((heads, tq, HEAD_DIM), F32), pltpu.VMEM((heads, tq, 1), F32)],
        compiler_params=_cparams("parallel", "arbitrary"),
        name="attn_prompt",
    )(bias, q, k, v)


def _attn_sample_kernel(pt_ref, bias_ref, q_ref, *refs, pp, page):
    k_refs, v_refs = refs[:pp], refs[pp:2 * pp]
    o_ref, acc_ref, carry_ref = refs[2 * pp:]
    j = pl.program_id(1)
    nh = N_HEADS

    @pl.when(j == 0)
    def _():
        acc_ref[...] = jnp.zeros_like(acc_ref)
        carry_ref[...] = jnp.zeros_like(carry_ref)

    q = q_ref[0]
    head = lax.broadcasted_iota(jnp.int32, (nh, LANES), 0)

    def head_rows(page_refs, h):
        return jnp.concatenate(
            [r[0, 0, pl.ds(h, page, stride=nh), :] for r in page_refs], axis=0).astype(BF16)

    z_pages = [jnp.zeros((nh, page), F32) for _ in range(pp)]
    for h in range(nh):
        z_h = _dot_nt(q, head_rows(k_refs, h))
        z_pages = [jnp.where(head == h, z_h[:, p * page:(p + 1) * page], z_pages[p]) for p in range(pp)]
    z = jnp.concatenate(z_pages, axis=0) + bias_ref[...]
    sp = _softplus2(z)
    spb = sp.astype(BF16)
    later = _dot(spb, _strict_lower(page))
    total = later[:, 0:1] + spb[:, 0:1].astype(F32)
    carry = carry_ref[...]
    carries = [None] * pp
    for p in reversed(range(pp)):
        carries[p] = carry
        carry = carry + total[p * nh:(p + 1) * nh]
    carry_ref[...] = carry
    a = jnp.exp2((z - sp) - later - jnp.concatenate(carries, axis=0))
    a_cat = jnp.concatenate([a[p * nh:(p + 1) * nh] for p in range(pp)], axis=1).astype(BF16)
    acc = acc_ref[...]
    for h in range(nh):
        acc = jnp.where(head == h, acc + _dot(a_cat, head_rows(v_refs, h)), acc)
    acc_ref[...] = acc

    @pl.when(j == pl.num_programs(1) - 1)
    def _():
        o_ref[0] = acc.astype(o_ref.dtype)


def attn_sample(q, cache_k, cache_v, layer, page_table, b_sb, pp=16):
    n_seq, n_pages = page_table.shape
    n_layers, n_pool, page = cache_k.shape[:3]
    pp = min(pp, n_pages)
    steps = n_pages // pp
    ck = cache_k.reshape(n_layers, n_pool, page * N_HEADS, HEAD_DIM)
    cv = cache_v.reshape(n_layers, n_pool, page * N_HEADS, HEAD_DIM)
    bias = jnp.broadcast_to((b_sb.astype(F32) * LOG2E)[None, :, None], (pp, N_HEADS, page)).reshape(pp * N_HEADS, page)

    def page_spec(p):
        def index(b, j, pt):
            return (layer, pt[b * n_pages + (steps - 1 - j) * pp + p], 0, 0)
        return pl.BlockSpec((1, 1, page * N_HEADS, HEAD_DIM), index)

    kern = functools.partial(_attn_sample_kernel, pp=pp, page=page)
    grid_spec = pltpu.PrefetchScalarGridSpec(
        num_scalar_prefetch=1,
        grid=(n_seq, steps),
        in_specs=[
            pl.BlockSpec((pp * N_HEADS, page), lambda b, j, pt: (0, 0)),
            pl.BlockSpec((1, N_HEADS, HEAD_DIM), lambda b, j, pt: (b, 0, 0)),
            *[page_spec(p) for p in range(pp)],
            *[page_spec(p) for p in range(pp)],
        ],
        out_specs=pl.BlockSpec((1, N_HEADS, HEAD_DIM), lambda b, j, pt: (b, 0, 0)),
        scratch_shapes=[pltpu.VMEM((N_HEADS, HEAD_DIM), F32), pltpu.VMEM((N_HEADS, 1), F32)],
    )
    return pl.pallas_call(
        kern,
        grid_spec=grid_spec,
        out_shape=jax.ShapeDtypeStruct((n_seq, N_HEADS, HEAD_DIM), BF16),
        compiler_params=_cparams("parallel", "arbitrary"),
        name="attn_sample",
    )(page_table.reshape(-1), bias, q, *([ck] * pp), *([cv] * pp))


def _merge_kernel(h_ref, act_ref, att_ref, wga_ref, wgb_ref, wco_ref, wao_ref, o_ref):
    h = h_ref[...]
    conv_out = _dot(act_ref[...], wco_ref[...])
    att_out = _dot(att_ref[...], wao_ref[...])
    ga = jax.nn.sigmoid(_dot(h, wga_ref[...]))
    gb = jax.nn.sigmoid(_dot(h, wgb_ref[...]))
    o_ref[...] = (ga * conv_out + gb * att_out).astype(o_ref.dtype)


def merge(h, act, att, w, layer, col_ga, col_gb, wco, wao, tm, tn=512):
    t, d = h.shape
    c = act.shape[1]
    n = wco.shape[1]
    return pl.pallas_call(
        _merge_kernel,
        grid=(t // tm, n // tn),
        in_specs=[
            pl.BlockSpec((tm, d), lambda i, j: (i, 0)),
            pl.BlockSpec((tm, c), lambda i, j: (i, 0)),
            pl.BlockSpec((tm, c), lambda i, j: (i, 0)),
            _cols(d, tn, layer, col_ga),
            _cols(d, tn, layer, col_gb),
            pl.BlockSpec((c, tn), lambda i, j: (0, j)),
            pl.BlockSpec((c, tn), lambda i, j: (0, j)),
        ],
        out_specs=pl.BlockSpec((tm, tn), lambda i, j: (i, j)),
        out_shape=jax.ShapeDtypeStruct((t, n), BF16),
        compiler_params=_cparams("parallel", "parallel"),
        name="merge",
    )(h, act, att, w, w, wco, wao)


def _first_argmax(x, m, lane):
    return jnp.min(jnp.where(x == m, lane, float(LANES)), axis=-1, keepdims=True)


def _router(logits):
    lane = lax.broadcasted_iota(jnp.int32, logits.shape, 1).astype(F32)
    neg = -jnp.inf
    lg = jnp.where(lane >= N_EXPERTS, jnp.where(lane < N_EXPERTS + N_GROUPS, logits, neg), neg)
    mg = jnp.max(lg, axis=-1, keepdims=True)
    p_grp = 1.0 / jnp.sum(jnp.exp(lg - mg), axis=-1, keepdims=True)
    g_idx = _first_argmax(lg, mg, lane) - N_EXPERTS
    in_group = jnp.floor(lane * (1.0 / EXPERTS_PER_GROUP)) == g_idx
    le = jnp.where(in_group, logits, neg)
    m1 = jnp.max(le, axis=-1, keepdims=True)
    i1 = _first_argmax(le, m1, lane)
    le2 = jnp.where(lane == i1, neg, le)
    m2 = jnp.max(le2, axis=-1, keepdims=True)
    i2 = _first_argmax(le2, m2, lane)
    e2 = jnp.exp(m2 - m1)
    w1 = p_grp / (1.0 + e2)
    w2 = w1 * e2
    gate = jnp.where(lane == i1, w1, 0.0) + jnp.where(lane == i2, w2, 0.0)
    route = jnp.where(lane == 0, i1, jnp.where(lane == 1, i2, jnp.where(lane == 2, w1, jnp.where(lane == 3, w2, 0.0))))
    return gate, route


def _split_dot(x_hi, x_lo, w_hi_ref, w_lo_ref):
    w_hi = w_hi_ref[...]
    return _dot(x_hi, w_hi) + (_dot(x_lo, w_hi) + _dot(x_hi, w_lo_ref[...]))


def _oproj_kernel(m_ref, wo_ref, x_ref, gt_ref, g2_ref, sc_ref, sh_ref, wrh_ref, wrl_ref, br_ref,
                  x1_ref, h2_ref, gate_ref, route_ref):
    x1 = x_ref[...] + gt_ref[...] * _dot(m_ref[...], wo_ref[...])
    x1_ref[...] = x1
    h2 = _rms_mod(x1, g2_ref[...], sc_ref[...], sh_ref[...])
    hi = h2.astype(BF16)
    h2_ref[...] = hi
    lo = (h2 - hi.astype(F32)).astype(BF16)
    gate_ref[...], route_ref[...] = _router(_split_dot(hi, lo, wrh_ref, wrl_ref) + br_ref[...])


def _pad_lanes(w):
    return jnp.pad(w, ((0, 0), (0, LANES - w.shape[-1])))


def _hi_lo(w):
    hi = w.astype(BF16)
    return hi, (w - hi.astype(F32)).astype(BF16)


def oproj_router(m, wo, x, gt, g2, sc, sh, w_rg, b_rg, w_re, b_re, tm):
    t, d = x.shape
    wrh, wrl = _hi_lo(_pad_lanes(jnp.concatenate([w_re, w_rg], axis=1)))
    br = _pad_lanes(jnp.concatenate([b_re, b_rg]).reshape(1, -1))
    full = lambda shape: pl.BlockSpec(shape, lambda i: (0, 0))
    return pl.pallas_call(
        _oproj_kernel,
        grid=(t // tm,),
        in_specs=[
            pl.BlockSpec((tm, d), lambda i: (i, 0)),
            full((d, d)),
            pl.BlockSpec((tm, d), lambda i: (i, 0)),
            _row_spec(gt.shape[0], tm, d),
            full((1, d)),
            _row_spec(sc.shape[0], tm, d),
            _row_spec(sh.shape[0], tm, d),
            full((d, LANES)), full((d, LANES)), full((1, LANES)),
        ],
        out_specs=[
            pl.BlockSpec((tm, d), lambda i: (i, 0)),
            pl.BlockSpec((tm, d), lambda i: (i, 0)),
            pl.BlockSpec((tm, LANES), lambda i: (i, 0)),
            pl.BlockSpec((tm, LANES), lambda i: (i, 0)),
        ],
        out_shape=[
            jax.ShapeDtypeStruct((t, d), F32),
            jax.ShapeDtypeStruct((t, d), BF16),
            jax.ShapeDtypeStruct((t, LANES), F32),
            jax.ShapeDtypeStruct((t, LANES), F32),
        ],
        compiler_params=_cparams("parallel"),
        name="oproj_router",
    )(m, wo, x, gt, g2.reshape(1, d), sc, sh, wrh, wrl, br)


def _moe_kernel(h_ref, gate_ref, wg_ref, wu_ref, wd_ref, x_ref, gt_ref, o_ref, wg16_ref, wu16_ref, wd16_ref, acc_ref):
    e = pl.program_id(0)

    @pl.when(e == 0)
    def _():
        acc_ref[...] = jnp.zeros_like(acc_ref)

    wg = wg_ref[0].astype(BF16)
    wu = wu_ref[0].astype(BF16)
    wd = wd_ref[0].astype(BF16)
    wg16_ref[0] = wg
    wu16_ref[0] = wu
    wd16_ref[0] = wd
    h = h_ref[...]
    gate = gate_ref[...]
    lane = lax.broadcasted_iota(jnp.int32, gate.shape, 1)
    ge = jnp.sum(jnp.where(lane == e, gate, 0.0), axis=-1, keepdims=True)
    act = (_silu(_dot(h, wg)) * _dot(h, wu)) * ge
    acc_ref[...] += _dot(act.astype(BF16), wd)

    @pl.when(e == pl.num_programs(0) - 1)
    def _():
        o_ref[...] = x_ref[...] + gt_ref[...] * acc_ref[...]


def moe_dense(h2, gate, w_eg, w_eu, w_ed, layer, x1, gt):
    t, d = x1.shape
    _, n_e, _, f = w_eg.shape
    whole = lambda width: pl.BlockSpec((t, width), lambda e: (0, 0))
    up_in = pl.BlockSpec((None, 1, d, f), lambda e: (layer, e, 0, 0))
    down_in = pl.BlockSpec((None, 1, f, d), lambda e: (layer, e, 0, 0))
    up = pl.BlockSpec((1, d, f), lambda e: (e, 0, 0))
    down = pl.BlockSpec((1, f, d), lambda e: (e, 0, 0))
    return pl.pallas_call(
        _moe_kernel,
        grid=(n_e,),
        in_specs=[whole(d), whole(LANES), up_in, up_in, down_in, whole(d),
                  pl.BlockSpec((gt.shape[0], d), lambda e: (0, 0))],
        out_specs=[whole(d), up, up, down],
        out_shape=[
            jax.ShapeDtypeStruct((t, d), F32),
            jax.ShapeDtypeStruct((n_e, d, f), BF16),
            jax.ShapeDtypeStruct((n_e, d, f), BF16),
            jax.ShapeDtypeStruct((n_e, f, d), BF16),
        ],
        scratch_shapes=[pltpu.VMEM((t, d), F32)],
        compiler_params=_cparams("arbitrary"),
        name="moe_dense",
    )(h2, gate, w_eg, w_eu, w_ed, x1, gt)


SEG_ALIGN = BF16_ROWS
EXPERT_ROWS = 128
STEP_EXPERTS = 2
SORT_ROWS = MXU_DIM


ROUTE_ROWS = 8


def _route_pos_kernel(route_ref, col_ref, row_ref, seg_ref):
    route = route_ref[...]
    tm = route.shape[0]
    lane = lax.broadcasted_iota(jnp.int32, route.shape, 1).astype(F32)
    oh1 = lane == route[:, 0:1]
    oh2 = lane == route[:, 1:2]
    both = jnp.where(oh1, 1.0, jnp.where(oh2, 1.0, 0.0))
    before = (lax.broadcasted_iota(jnp.int32, (tm, tm), 1) < lax.broadcasted_iota(jnp.int32, (tm, tm), 0))
    earlier = _dot(jnp.where(before, 1.0, 0.0).astype(BF16), both.astype(BF16))
    units = jnp.floor((jnp.sum(both, axis=0, keepdims=True) + (SEG_ALIGN - 1)) * (1.0 / SEG_ALIGN))
    lower = (lax.broadcasted_iota(jnp.int32, (LANES, LANES), 0) < lax.broadcasted_iota(jnp.int32, (LANES, LANES), 1))
    start = SEG_ALIGN * _dot(jnp.broadcast_to(units, (ROUTE_ROWS, LANES)).astype(BF16),
                             jnp.where(lower, 1.0, 0.0).astype(BF16))[0:1]
    size = SEG_ALIGN * units
    spot = earlier + start
    pos1 = jnp.sum(jnp.where(oh1, spot, 0.0), axis=-1, keepdims=True)
    pos2 = jnp.sum(jnp.where(oh2, spot, 0.0), axis=-1, keepdims=True)
    col = jnp.where(lane == 0, pos1, jnp.where(lane == 1, pos2, route))
    col_ref[...] = col
    row_ref[0] = jnp.transpose(col)[0:ROUTE_ROWS, :]
    sub = lax.broadcasted_iota(jnp.int32, (ROUTE_ROWS, LANES), 0)
    trips = jnp.floor((size + (EXPERT_ROWS - 1)) * (1.0 / EXPERT_ROWS))
    seg_ref[0] = jnp.where(sub == 0, start, jnp.where(sub == 1, start + size, jnp.where(sub == 2, trips, 0.0)))


def route_pos(route, tm):
    t = route.shape[0]
    nt = t // tm
    col, row, seg = pl.pallas_call(
        _route_pos_kernel,
        grid=(nt,),
        in_specs=[pl.BlockSpec((tm, LANES), lambda i: (i, 0))],
        out_specs=[
            pl.BlockSpec((tm, LANES), lambda i: (i, 0)),
            pl.BlockSpec((1, ROUTE_ROWS, tm), lambda i: (i, 0, 0)),
            pl.BlockSpec((1, ROUTE_ROWS, LANES), lambda i: (i, 0, 0)),
        ],
        out_shape=[
            jax.ShapeDtypeStruct((t, LANES), F32),
            jax.ShapeDtypeStruct((nt, ROUTE_ROWS, tm), F32),
            jax.ShapeDtypeStruct((nt, ROUTE_ROWS, LANES), F32),
        ],
        compiler_params=_cparams("parallel"),
        name="route_pos",
    )(route)
    seg = seg[:, 0:3, 0:N_EXPERTS].astype(jnp.int32)
    return col, row, seg[:, 0].reshape(-1), seg[:, 1].reshape(-1), seg[:, 2].reshape(-1)


def _moe_routed_kernel(start_ref, end_ref, trips_ref, h_ref, col_ref, row_ref, wg_ref, wu_ref, wd_ref, o_ref,
                       xy_ref, wrow_ref):
    i = pl.program_id(0)
    e = pl.program_id(1)
    n_e = pl.num_programs(1)
    tm = h_ref.shape[0]
    rows = xy_ref.shape[0]

    @pl.when(e == 0)
    def _():
        pos1, pos2 = row_ref[0, 0:1, :], row_ref[0, 1:2, :]
        w1, w2 = row_ref[0, 2:3, :], row_ref[0, 3:4, :]

        def sort(c, carry):
            r0 = pl.multiple_of(c * SORT_ROWS, SORT_ROWS)
            row = (r0 + lax.broadcasted_iota(jnp.int32, (SORT_ROWS, tm), 0)).astype(F32)
            hit1 = pos1 == row
            hit2 = pos2 == row
            pick = jnp.where(hit1, 1.0, jnp.where(hit2, 1.0, 0.0)).astype(BF16)
            xy_ref[pl.ds(r0, SORT_ROWS), :] = _dot(pick, h_ref[...]).astype(BF16)
            wrow_ref[pl.ds(r0, SORT_ROWS), :] = jnp.sum(
                jnp.where(hit1, w1, jnp.where(hit2, w2, 0.0)), axis=-1, keepdims=True)
            return carry
        lax.fori_loop(0, rows // SORT_ROWS, sort, 0)

    for k in range(STEP_EXPERTS):
        seg = (i * n_e + e) * STEP_EXPERTS + k
        seg_start = start_ref[seg]
        seg_end = end_ref[seg]

        def expert(c, carry, k=k, seg_start=seg_start, seg_end=seg_end):
            r0 = pl.multiple_of(seg_start + c * EXPERT_ROWS, SEG_ALIGN)
            xs = xy_ref[pl.ds(r0, EXPERT_ROWS), :]
            w_row = wrow_ref[pl.ds(r0, EXPERT_ROWS), :]
            act = (_silu(_dot(xs, wg_ref[k])) * _dot(xs, wu_ref[k])) * w_row
            y = _dot(act.astype(BF16), wd_ref[k]).astype(BF16)
            row = r0 + lax.broadcasted_iota(jnp.int32, (EXPERT_ROWS, 1), 0)
            xy_ref[pl.ds(r0, EXPERT_ROWS), :] = jnp.where(row < seg_end, y, xs)
            return carry

        lax.fori_loop(0, trips_ref[seg], expert, 0)

    @pl.when(e == n_e - 1)
    def _():
        def combine(c, carry):
            t0 = pl.multiple_of(c * SORT_ROWS, SORT_ROWS)
            row = lax.broadcasted_iota(jnp.int32, (SORT_ROWS, rows), 1).astype(F32)
            hit1 = col_ref[pl.ds(t0, SORT_ROWS), 0:1] == row
            hit2 = col_ref[pl.ds(t0, SORT_ROWS), 1:2] == row
            pick = jnp.where(hit1, 1.0, jnp.where(hit2, 1.0, 0.0)).astype(BF16)
            o_ref[pl.ds(t0, SORT_ROWS), :] = _dot(pick, xy_ref[...])
            return carry
        lax.fori_loop(0, tm // SORT_ROWS, combine, 0)


def moe_routed(h2, route, w_eg, w_eu, w_ed, tm):
    t, d = h2.shape
    n_e, _, f = w_eg.shape
    rows = _round_up(2 * tm + n_e * (SEG_ALIGN - 1) + EXPERT_ROWS, SORT_ROWS)
    col, row, start, end, trips = route_pos(route, tm)
    grid_spec = pltpu.PrefetchScalarGridSpec(
        num_scalar_prefetch=3,
        grid=(t // tm, n_e // STEP_EXPERTS),
        in_specs=[
            pl.BlockSpec((tm, d), lambda i, e, *_: (i, 0)),
            pl.BlockSpec((tm, LANES), lambda i, e, *_: (i, 0)),
            pl.BlockSpec((1, ROUTE_ROWS, tm), lambda i, e, *_: (i, 0, 0)),
            pl.BlockSpec((STEP_EXPERTS, d, f), lambda i, e, *_: (e, 0, 0)),
            pl.BlockSpec((STEP_EXPERTS, d, f), lambda i, e, *_: (e, 0, 0)),
            pl.BlockSpec((STEP_EXPERTS, f, d), lambda i, e, *_: (e, 0, 0)),
        ],
        out_specs=pl.BlockSpec((tm, d), lambda i, e, *_: (i, 0)),
        scratch_shapes=[pltpu.VMEM((rows, d), BF16), pltpu.VMEM((rows, 1), F32)],
    )
    return pl.pallas_call(
        _moe_routed_kernel,
        grid_spec=grid_spec,
        out_shape=jax.ShapeDtypeStruct((t, d), F32),
        compiler_params=_cparams("parallel", "arbitrary", vmem=VMEM_LIMIT_MOE),
        name="moe_routed",
    )(start, end, trips, h2, col, row, w_eg, w_eu, w_ed)


def _mix_rows(x, h, mods, wts, layer, tm, k_all, v_all, conv_fn, attn_fn):
    _, _, gt1, sh2, sc2, _ = mods
    d = x.shape[1]
    w_in = wts["w_in"]
    c_ch = wts["w_co"].shape[0]
    attn_w = wts["w_ao"].shape[0]
    col_q = 2 * c_ch
    col_g = col_q + 3 * attn_w
    u = glu_proj(h, w_in, layer, 0, c_ch, c_ch, tm)
    q, k_all, k16, v_all, v16 = qkv_proj(h, w_in, layer, col_q, attn_w, k_all, v_all, tm)
    act, new_conv = conv_fn(u)
    att = attn_fn(q, k16, v16)
    m = merge(h, act, att, w_in, layer, col_g, col_g + d, wts["w_co"], wts["w_ao"], tm)
    x1, h2, gate, route = oproj_router(m, wts["w_o"], x, gt1, wts["g_n2"], sc2, sh2,
                                       wts["w_rg"], wts["b_rg"], wts["w_re"], wts["b_re"], min(tm, 256))
    return x1, h2, gate, route, k_all, v_all, new_conv


def kernel(x_prompt, x_sample, cache_k, cache_v, state_conv, page_table, c_prompt, c_sample, w_ada, b_ada, g_n1, w_in, b_sb, w_dw, b_dw, ln_g, ln_b, w_conv_out, w_attn_out, w_o, g_n2, w_rg, b_rg, w_re, b_re, w_eg, w_eu, w_ed, g_final):
    batch, seq, d = x_prompt.shape
    n_seq, t_new, _ = x_sample.shape
    assert batch == 1 and t_new == 1, "one prompt sequence and one new token per sample sequence"
    n_layers = w_in.shape[0]
    c_ch = w_dw.shape[-1]
    attn_w = N_HEADS * HEAD_DIM
    hist = CONV_KERNEL - 1
    assert seq >= hist

    mod_rows = n_seq + BF16_ROWS
    c_all = jnp.zeros((mod_rows, d), F32).at[:n_seq].set(c_sample).at[n_seq:n_seq + 1].set(c_prompt)
    mod = ada_mod(c_all, w_ada, b_ada)

    tm_p = min(1024, seq)
    tm_n = min(512, seq)
    xp = x_prompt.reshape(seq, d)
    xs = x_sample.reshape(n_seq, d)
    pending = None
    zero_halo = jnp.zeros((HALO, c_ch), F32)
    zeros_row = jnp.zeros((1, d), F32)
    kp = jnp.zeros((n_layers, seq, attn_w), F32)
    vp = jnp.zeros((n_layers, seq, attn_w), F32)
    ks = jnp.zeros((n_layers, n_seq, attn_w), F32)
    vs = jnp.zeros((n_layers, n_seq, attn_w), F32)
    cs = jnp.zeros(state_conv.shape, F32)
    cp = []
    w_in16 = w_in.astype(BF16)
    for l in range(n_layers):
        wts = dict(
            g_n2=g_n2[l], w_in=w_in16,
            w_co=w_conv_out[l].astype(BF16), w_ao=w_attn_out[l].astype(BF16), w_o=w_o[l].astype(BF16),
            w_rg=w_rg[l], b_rg=b_rg[l], w_re=w_re[l], b_re=b_re[l],
        )
        mods_s = tuple(mod[l, :n_seq, i * d:(i + 1) * d] for i in range(6))
        mods_p = tuple(mod[l, n_seq:n_seq + 1, i * d:(i + 1) * d] for i in range(6))

        def conv_p(u, l=l):
            act = conv_prompt(u, zero_halo, w_dw[l], b_dw[l], ln_g[l], ln_b[l])
            return act, u[seq - hist:]

        def attn_p(q, k16, v16, l=l):
            return attn_prompt(q, k16, v16, b_sb[l])

        def conv_s(u, l=l, cs=cs):
            return conv_sample(u, state_conv, l, w_dw[l], b_dw[l], ln_g[l], ln_b[l], cs)

        def attn_s(q, k16, v16, l=l):
            att = attn_sample(q.reshape(n_seq, N_HEADS, HEAD_DIM), cache_k, cache_v, l, page_table, b_sb[l])
            return att.reshape(n_seq, attn_w)

        hs = norm_mod(xs, g_n1[l], mods_s[1], mods_s[0], BF16, n_seq)
        x1s, h2s, gate_s, _, ks, vs, cs = _mix_rows(xs, hs, mods_s, wts, l, n_seq, ks, vs, conv_s, attn_s)
        xs, weg, weu, wed = moe_dense(h2s, gate_s, w_eg, w_eu, w_ed, l, x1s, mods_s[5])

        if pending is None:
            hp = norm_mod(xp, g_n1[l], mods_p[1], mods_p[0], BF16, tm_p)
        else:
            xp, hp = resid_norm(*pending, g_n1[l], mods_p[1], mods_p[0], BF16, tm_n)
        x1, h2, _, route, kp, vp, cp_l = _mix_rows(xp, hp, mods_p, wts, l, tm_p, kp, vp, conv_p, attn_p)
        pending = (x1, moe_routed(h2, route, weg, weu, wed, tm_p), mods_p[5])
        cp.append(cp_l.reshape(1, hist, c_ch))

    _, y_prompt = resid_norm(*pending, g_final, zeros_row, zeros_row, F32, tm_n)
    y_sample = norm_mod(xs, g_final, zeros_row, zeros_row, F32, n_seq)
    return (y_prompt.reshape(1, seq, d), y_sample.reshape(n_seq, 1, d),
            kp.reshape(n_layers, 1, seq, N_HEADS, HEAD_DIM), vp.reshape(n_layers, 1, seq, N_HEADS, HEAD_DIM),
            jnp.stack(cp),
            ks.reshape(n_layers, n_seq, 1, N_HEADS, HEAD_DIM), vs.reshape(n_layers, n_seq, 1, N_HEADS, HEAD_DIM), cs)
```

```python
import functools

import jax
import jax.numpy as jnp
from jax import lax
from jax.experimental import pallas as pl
from jax.experimental.pallas import tpu as pltpu

F32 = jnp.float32
BF16 = jnp.bfloat16

N_HEADS = 8
HEAD_DIM = 128
CONV_KERNEL = 31
N_GROUPS = 4
EXPERTS_PER_GROUP = 8
N_EXPERTS = N_GROUPS * EXPERTS_PER_GROUP
EPS = 1e-6

LOG2E = 1.4426950408889634
Q_SCALE = HEAD_DIM ** -0.5 * LOG2E

LANES = 128
SUBLANES = 8
BF16_ROWS = 16
MXU_DIM = 256
V7X_VMEM_BYTES = 64 * 1024 * 1024
VMEM_LIMIT = V7X_VMEM_BYTES - 8 * 1024 * 1024
VMEM_LIMIT_MOE = V7X_VMEM_BYTES - 4 * 1024 * 1024


def _cparams(*sem, vmem=VMEM_LIMIT):
    return pltpu.CompilerParams(dimension_semantics=sem, vmem_limit_bytes=vmem)


def _dot(a, b):
    return jnp.dot(a, b, preferred_element_type=F32)


def _dot_nt(a, b):
    return lax.dot_general(a, b, (((1,), (1,)), ((), ())), preferred_element_type=F32)


def _silu(x):
    return x * jax.nn.sigmoid(x)


def _round_up(x, m):
    return (x + m - 1) // m * m


def _row_spec(rows, tm, width):
    if rows == 1:
        return pl.BlockSpec((1, width), lambda i, *_: (0, 0))
    return pl.BlockSpec((tm, width), lambda i, *_: (i, 0))


def _ada_kernel(c_ref, w_ref, b_ref, o_ref):
    a = _silu(c_ref[...]).astype(BF16)
    o_ref[0] = _dot(a, w_ref[0].astype(BF16)) + b_ref[0]


def ada_mod(c_all, w_ada, b_ada, tn=512):
    n_layers, d, n = w_ada.shape
    rows = c_all.shape[0]
    return pl.pallas_call(
        _ada_kernel,
        grid=(n_layers, n // tn),
        in_specs=[
            pl.BlockSpec((rows, d), lambda l, j: (0, 0)),
            pl.BlockSpec((1, d, tn), lambda l, j: (l, 0, j)),
            pl.BlockSpec((1, 1, tn), lambda l, j: (l, 0, j)),
        ],
        out_specs=pl.BlockSpec((1, rows, tn), lambda l, j: (l, 0, j)),
        out_shape=jax.ShapeDtypeStruct((n_layers, rows, n), F32),
        compiler_params=_cparams("parallel", "parallel"),
        name="ada_mod",
    )(c_all, w_ada, b_ada.reshape(n_layers, 1, n))


def _rms_mod(x, g, sc, sh):
    y = x * lax.rsqrt(jnp.mean(x * x, axis=-1, keepdims=True) + EPS)
    return (y * g) * (1.0 + sc) + sh


def _norm_mod_kernel(x_ref, g_ref, sc_ref, sh_ref, o_ref):
    o_ref[...] = _rms_mod(x_ref[...], g_ref[...], sc_ref[...], sh_ref[...]).astype(o_ref.dtype)


def norm_mod(x, g, sc, sh, out_dtype, tm):
    t, d = x.shape
    return pl.pallas_call(
        _norm_mod_kernel,
        grid=(t // tm,),
        in_specs=[
            pl.BlockSpec((tm, d), lambda i: (i, 0)),
            pl.BlockSpec((1, d), lambda i: (0, 0)),
            _row_spec(sc.shape[0], tm, d),
            _row_spec(sh.shape[0], tm, d),
        ],
        out_specs=pl.BlockSpec((tm, d), lambda i: (i, 0)),
        out_shape=jax.ShapeDtypeStruct((t, d), out_dtype),
        compiler_params=_cparams("parallel"),
        name="norm_mod",
    )(x, g.reshape(1, d), sc, sh)


def _resid_norm_kernel(x_ref, m_ref, gt_ref, g_ref, sc_ref, sh_ref, x2_ref, o_ref):
    x = x_ref[...] + gt_ref[...] * m_ref[...]
    x2_ref[...] = x
    o_ref[...] = _rms_mod(x, g_ref[...], sc_ref[...], sh_ref[...]).astype(o_ref.dtype)


def resid_norm(x, m, gt, g, sc, sh, out_dtype, tm):
    t, d = x.shape
    return pl.pallas_call(
        _resid_norm_kernel,
        grid=(t // tm,),
        in_specs=[
            pl.BlockSpec((tm, d), lambda i: (i, 0)),
            pl.BlockSpec((tm, d), lambda i: (i, 0)),
            _row_spec(gt.shape[0], tm, d),
            pl.BlockSpec((1, d), lambda i: (0, 0)),
            _row_spec(sc.shape[0], tm, d),
            _row_spec(sh.shape[0], tm, d),
        ],
        out_specs=[pl.BlockSpec((tm, d), lambda i: (i, 0)), pl.BlockSpec((tm, d), lambda i: (i, 0))],
        out_shape=[jax.ShapeDtypeStruct((t, d), F32), jax.ShapeDtypeStruct((t, d), out_dtype)],
        compiler_params=_cparams("parallel"),
        name="resid_norm",
    )(x, m, gt, g.reshape(1, d), sc, sh)


def _glu_kernel(h_ref, wa_ref, wb_ref, u_ref):
    h = h_ref[...]
    u_ref[...] = _dot(h, wa_ref[...]) * jax.nn.sigmoid(_dot(h, wb_ref[...]))


def _cols(d, tn, layer, col0):
    first = col0 // tn
    assert first * tn == col0
    return pl.BlockSpec((None, d, tn), lambda i, j: (layer, 0, first + j))


def glu_proj(h, w, layer, col_a, col_b, n, tm, tn=512):
    t, d = h.shape
    return pl.pallas_call(
        _glu_kernel,
        grid=(t // tm, n // tn),
        in_specs=[pl.BlockSpec((tm, d), lambda i, j: (i, 0)),
                  _cols(d, tn, layer, col_a), _cols(d, tn, layer, col_b)],
        out_specs=pl.BlockSpec((tm, tn), lambda i, j: (i, j)),
        out_shape=jax.ShapeDtypeStruct((t, n), F32),
        compiler_params=_cparams("parallel", "parallel"),
        name="glu_proj",
    )(h, w, w)


def _qkv_kernel(h_ref, wq_ref, wk_ref, wv_ref, k_all_ref, v_all_ref, q_ref, k32_ref, k16_ref, v32_ref, v16_ref):
    del k_all_ref, v_all_ref
    h = h_ref[...]
    q_ref[...] = (_dot(h, wq_ref[...]) * Q_SCALE).astype(q_ref.dtype)
    k = _dot(h, wk_ref[...])
    k32_ref[...] = k
    k16_ref[...] = k.astype(k16_ref.dtype)
    v = _dot(h, wv_ref[...])
    v32_ref[...] = v
    v16_ref[...] = v.astype(v16_ref.dtype)


def qkv_proj(h, w, layer, col_q, n, k_all, v_all, tm, tn=512):
    t, d = h.shape
    out = pl.BlockSpec((tm, tn), lambda i, j: (i, j))
    out_l = pl.BlockSpec((None, tm, tn), lambda i, j: (layer, i, j))
    stacked = jax.ShapeDtypeStruct(k_all.shape, F32)
    flat = lambda dt: jax.ShapeDtypeStruct((t, n), dt)
    return pl.pallas_call(
        _qkv_kernel,
        grid=(t // tm, n // tn),
        in_specs=[pl.BlockSpec((tm, d), lambda i, j: (i, 0)),
                  _cols(d, tn, layer, col_q), _cols(d, tn, layer, col_q + n), _cols(d, tn, layer, col_q + 2 * n),
                  pl.BlockSpec(memory_space=pl.ANY), pl.BlockSpec(memory_space=pl.ANY)],
        out_specs=[out, out_l, out, out_l, out],
        out_shape=[flat(BF16), stacked, flat(BF16), stacked, flat(BF16)],
        input_output_aliases={4: 1, 5: 3},
        compiler_params=_cparams("parallel", "parallel"),
        name="qkv_proj",
    )(h, w, w, w, k_all, v_all)


def _layer_norm_silu(dc, g, b):
    mu = jnp.mean(dc, axis=-1, keepdims=True)
    xc = dc - mu
    var = jnp.mean(xc * xc, axis=-1, keepdims=True)
    return _silu((xc * lax.rsqrt(var + EPS)) * g + b)


HALO = 32
CONV_ROWS = 128


def _conv_prompt_kernel(u_ref, halo_ref, prev_ref, w_ref, b_ref, g_ref, beta_ref, o_ref, buf_ref, dc_ref, win_ref):
    i = pl.program_id(0)
    tt, c = u_ref.shape

    @pl.when(i == 0)
    def _():
        buf_ref[0:HALO, :] = prev_ref[...]

    @pl.when(i > 0)
    def _():
        buf_ref[0:HALO, :] = halo_ref[...]

    buf_ref[HALO:HALO + tt, :] = u_ref[...]
    first = HALO - (CONV_KERNEL - 1)
    by_phase = {}
    for j in range(CONV_KERNEL):
        by_phase.setdefault((first + j) % SUBLANES, []).append(first + j)
    rows = min(tt, CONV_ROWS)

    def lane_chunk(cc, carry):
        ls = pl.ds(pl.multiple_of(cc * LANES, LANES), LANES)
        for r0 in range(0, tt, rows):
            acc = jnp.broadcast_to(b_ref[:, ls], (rows, LANES))
            for offsets in by_phase.values():
                lo, hi = offsets[0], offsets[-1]
                win_ref[0:hi - lo + rows, :] = buf_ref[r0 + lo:r0 + hi + rows, ls]
                for o in offsets:
                    j = o - first
                    acc = acc + w_ref[j:j + 1, ls] * win_ref[o - lo:o - lo + rows, :]
            dc_ref[r0:r0 + rows, ls] = acc
        return carry

    lax.fori_loop(0, c // LANES, lane_chunk, 0)
    o_ref[...] = _layer_norm_silu(dc_ref[...], g_ref[...], beta_ref[...]).astype(o_ref.dtype)


def conv_prompt(u, prev_halo, w_dw, b_dw, ln_g, ln_b, tt=256):
    t, c = u.shape
    ratio = tt // HALO
    return pl.pallas_call(
        _conv_prompt_kernel,
        grid=(t // tt,),
        in_specs=[
            pl.BlockSpec((tt, c), lambda i: (i, 0)),
            pl.BlockSpec((HALO, c), lambda i: (jnp.maximum(i * ratio - 1, 0), 0)),
            pl.BlockSpec((HALO, c), lambda i: (0, 0)),
            pl.BlockSpec((CONV_KERNEL, c), lambda i: (0, 0)),
            pl.BlockSpec((1, c), lambda i: (0, 0)),
            pl.BlockSpec((1, c), lambda i: (0, 0)),
            pl.BlockSpec((1, c), lambda i: (0, 0)),
        ],
        out_specs=pl.BlockSpec((tt, c), lambda i: (i, 0)),
        out_shape=jax.ShapeDtypeStruct((t, c), BF16),
        scratch_shapes=[pltpu.VMEM((HALO + tt, c), F32), pltpu.VMEM((tt, c), F32),
                        pltpu.VMEM((HALO + min(tt, CONV_ROWS), LANES), F32)],
        compiler_params=_cparams("arbitrary"),
        name="conv_prompt",
    )(u, u, prev_halo, w_dw, b_dw.reshape(1, c), ln_g.reshape(1, c), ln_b.reshape(1, c))


def _conv_sample_kernel(u_ref, prev_ref, w_ref, b_ref, g_ref, beta_ref, new_all_ref, act_ref, new_ref, dc_ref):
    del new_all_ref
    bb = u_ref.shape[0]
    hist = CONV_KERNEL - 1
    w_hist = w_ref[0:hist, :]
    w_last = w_ref[hist:hist + 1, :]
    for b in range(bb):
        prev = prev_ref[b]
        u_row = u_ref[b:b + 1, :]
        dc_ref[b:b + 1, :] = (jnp.sum(prev * w_hist, axis=0, keepdims=True) + w_last * u_row + b_ref[...])
        new_ref[b, 0:hist - 1, :] = prev_ref[b, 1:hist, :]
        new_ref[b, hist - 1:hist, :] = u_row
    act_ref[...] = _layer_norm_silu(dc_ref[...], g_ref[...], beta_ref[...]).astype(act_ref.dtype)


def conv_sample(u, state, layer, w_dw, b_dw, ln_g, ln_b, new_all, bb=8):
    n, c = u.shape
    hist = state.shape[2]
    hist_spec = pl.BlockSpec((None, bb, hist, c), lambda i: (layer, i, 0, 0))
    return pl.pallas_call(
        _conv_sample_kernel,
        grid=(n // bb,),
        in_specs=[
            pl.BlockSpec((bb, c), lambda i: (i, 0)),
            hist_spec,
            pl.BlockSpec((CONV_KERNEL, c), lambda i: (0, 0)),
            pl.BlockSpec((1, c), lambda i: (0, 0)),
            pl.BlockSpec((1, c), lambda i: (0, 0)),
            pl.BlockSpec((1, c), lambda i: (0, 0)),
            pl.BlockSpec(memory_space=pl.ANY),
        ],
        out_specs=[pl.BlockSpec((bb, c), lambda i: (i, 0)), hist_spec],
        out_shape=[jax.ShapeDtypeStruct((n, c), BF16), jax.ShapeDtypeStruct(new_all.shape, F32)],
        input_output_aliases={6: 1},
        scratch_shapes=[pltpu.VMEM((bb, c), F32)],
        compiler_params=_cparams("parallel"),
        name="conv_sample",
    )(u, state, w_dw, b_dw.reshape(1, c), ln_g.reshape(1, c), ln_b.reshape(1, c), new_all)


SOFTPLUS_CAP = 64.0


def _softplus2(z2):
    return jnp.maximum(z2, jnp.log(1.0 + jnp.exp2(jnp.minimum(z2, SOFTPLUS_CAP))) * LOG2E)


def _strict_lower(n):
    return (lax.broadcasted_iota(jnp.int32, (n, n), 0) > lax.broadcasted_iota(jnp.int32, (n, n), 1)).astype(BF16)


def _attn_prompt_kernel(bias_ref, q_ref, k_ref, v_ref, o_ref, acc_ref, carry_ref, *, tk, heads):
    qi = pl.program_id(1)
    tq = q_ref.shape[0]
    n_diag = tq // tk
    upper = _strict_lower(tk)
    acc_ref[...] = jnp.zeros_like(acc_ref)
    carry_ref[...] = jnp.zeros_like(carry_ref)

    def tile(hh, kb, masked):
        cols = slice(hh * HEAD_DIM, (hh + 1) * HEAD_DIM)
        start = pl.multiple_of(kb * tk, tk)
        ks = k_ref[pl.ds(start, tk), cols]
        vs = v_ref[pl.ds(start, tk), cols]
        z = _dot_nt(q_ref[:, cols], ks) + bias_ref[hh]
        sp = _softplus2(z)
        ls = z - sp
        if masked:
            t_pos = qi * tq + lax.broadcasted_iota(jnp.int32, (tq, tk), 0)
            s_pos = kb * tk + lax.broadcasted_iota(jnp.int32, (tq, tk), 1)
            valid = s_pos < t_pos
            sp = jnp.where(valid, sp, 0.0)
        spb = sp.astype(BF16)
        later = _dot(spb, upper)
        carry = carry_ref[hh]
        a = jnp.exp2(ls - later)
        if masked:
            a = jnp.where(valid, a, 0.0)
        acc_ref[hh] += _dot(a.astype(BF16), vs) * jnp.exp2(-carry)
        carry_ref[hh] = carry + later[:, 0:1] + spb[:, 0:1].astype(F32)

    for d in range(n_diag):
        for hh in range(heads):
            tile(hh, qi * n_diag + (n_diag - 1 - d), True)

    def body(j, c):
        for d in range(n_diag):
            for hh in range(heads):
                tile(hh, (qi - j) * n_diag - 1 - d, False)
        return c

    lax.fori_loop(0, qi, body, 0)
    for hh in range(heads):
        o_ref[:, hh * HEAD_DIM:(hh + 1) * HEAD_DIM] = acc_ref[hh].astype(o_ref.dtype)


def attn_prompt(q, k, v, b_sb, tq=512, tk=MXU_DIM, heads=4):
    t = q.shape[0]
    tq = min(tq, t)
    width = heads * HEAD_DIM
    bias = jnp.broadcast_to((b_sb.astype(F32) * LOG2E)[:, None, None], (N_HEADS, 1, tk))
    kern = functools.partial(_attn_prompt_kernel, tk=tk, heads=heads)
    return pl.pallas_call(
        kern,
        grid=(N_HEADS // heads, t // tq),
        in_specs=[
            pl.BlockSpec((heads, 1, tk), lambda g, i: (g, 0, 0)),
            pl.BlockSpec((tq, width), lambda g, i: (i, g)),
            pl.BlockSpec((t, width), lambda g, i: (0, g)),
            pl.BlockSpec((t, width), lambda g, i: (0, g)),
        ],
        out_specs=pl.BlockSpec((tq, width), lambda g, i: (i, g)),
        out_shape=jax.ShapeDtypeStruct(q.shape, BF16),
        scratch_shapes=[pltpu.VMEM((heads, tq, HEAD_DIM), F32), pltpu.VMEM((heads, tq, 1), F32)],
        compiler_params=_cparams("parallel", "arbitrary"),
        name="attn_prompt",
    )(bias, q, k, v)


def _attn_sample_kernel(pt_ref, bias_ref, q_ref, *refs, pp, page):
    k_refs, v_refs = refs[:pp], refs[pp:2 * pp]
    o_ref, acc_ref, carry_ref = refs[2 * pp:]
    j = pl.program_id(1)
    nh = N_HEADS

    @pl.when(j == 0)
    def _():
        acc_ref[...] = jnp.zeros_like(acc_ref)
        carry_ref[...] = jnp.zeros_like(carry_ref)

    q = q_ref[0]
    head = lax.broadcasted_iota(jnp.int32, (nh, LANES), 0)

    def head_rows(page_refs, h):
        return jnp.concatenate(
            [r[0, 0, pl.ds(h, page, stride=nh), :] for r in page_refs], axis=0).astype(BF16)

    z_pages = [jnp.zeros((nh, page), F32) for _ in range(pp)]
    for h in range(nh):
        z_h = _dot_nt(q, head_rows(k_refs, h))
        z_pages = [jnp.where(head == h, z_h[:, p * page:(p + 1) * page], z_pages[p]) for p in range(pp)]
    z = jnp.concatenate(z_pages, axis=0) + bias_ref[...]
    sp = _softplus2(z)
    spb = sp.astype(BF16)
    later = _dot(spb, _strict_lower(page))
    total = later[:, 0:1] + spb[:, 0:1].astype(F32)
    carry = carry_ref[...]
    carries = [None] * pp
    for p in reversed(range(pp)):
        carries[p] = carry
        carry = carry + total[p * nh:(p + 1) * nh]
    carry_ref[...] = carry
    a = jnp.exp2((z - sp) - later - jnp.concatenate(carries, axis=0))
    a_cat = jnp.concatenate([a[p * nh:(p + 1) * nh] for p in range(pp)], axis=1).astype(BF16)
    acc = acc_ref[...]
    for h in range(nh):
        acc = jnp.where(head == h, acc + _dot(a_cat, head_rows(v_refs, h)), acc)
    acc_ref[...] = acc

    @pl.when(j == pl.num_programs(1) - 1)
    def _():
        o_ref[0] = acc.astype(o_ref.dtype)


def attn_sample(q, cache_k, cache_v, layer, page_table, b_sb, pp=16):
    n_seq, n_pages = page_table.shape
    n_layers, n_pool, page = cache_k.shape[:3]
    pp = min(pp, n_pages)
    steps = n_pages // pp
    ck = cache_k.reshape(n_layers, n_pool, page * N_HEADS, HEAD_DIM)
    cv = cache_v.reshape(n_layers, n_pool, page * N_HEADS, HEAD_DIM)
    bias = jnp.broadcast_to((b_sb.astype(F32) * LOG2E)[None, :, None], (pp, N_HEADS, page)).reshape(pp * N_HEADS, page)

    def page_spec(p):
        def index(b, j, pt):
            return (layer, pt[b * n_pages + (steps - 1 - j) * pp + p], 0, 0)
        return pl.BlockSpec((1, 1, page * N_HEADS, HEAD_DIM), index)

    kern = functools.partial(_attn_sample_kernel, pp=pp, page=page)
    grid_spec = pltpu.PrefetchScalarGridSpec(
        num_scalar_prefetch=1,
        grid=(n_seq, steps),
        in_specs=[
            pl.BlockSpec((pp * N_HEADS, page), lambda b, j, pt: (0, 0)),
            pl.BlockSpec((1, N_HEADS, HEAD_DIM), lambda b, j, pt: (b, 0, 0)),
            *[page_spec(p) for p in range(pp)],
            *[page_spec(p) for p in range(pp)],
        ],
        out_specs=pl.BlockSpec((1, N_HEADS, HEAD_DIM), lambda b, j, pt: (b, 0, 0)),
        scratch_shapes=[pltpu.VMEM((N_HEADS, HEAD_DIM), F32), pltpu.VMEM((N_HEADS, 1), F32)],
    )
    return pl.pallas_call(
        kern,
        grid_spec=grid_spec,
        out_shape=jax.ShapeDtypeStruct((n_seq, N_HEADS, HEAD_DIM), BF16),
        compiler_params=_cparams("parallel", "arbitrary"),
        name="attn_sample",
    )(page_table.reshape(-1), bias, q, *([ck] * pp), *([cv] * pp))


def _merge_kernel(h_ref, act_ref, att_ref, wga_ref, wgb_ref, wco_ref, wao_ref, o_ref):
    h = h_ref[...]
    conv_out = _dot(act_ref[...], wco_ref[...])
    att_out = _dot(att_ref[...], wao_ref[...])
    ga = jax.nn.sigmoid(_dot(h, wga_ref[...]))
    gb = jax.nn.sigmoid(_dot(h, wgb_ref[...]))
    o_ref[...] = (ga * conv_out + gb * att_out).astype(o_ref.dtype)


def merge(h, act, att, w, layer, col_ga, col_gb, wco, wao, tm, tn=512):
    t, d = h.shape
    c = act.shape[1]
    n = wco.shape[1]
    return pl.pallas_call(
        _merge_kernel,
        grid=(t // tm, n // tn),
        in_specs=[
            pl.BlockSpec((tm, d), lambda i, j: (i, 0)),
            pl.BlockSpec((tm, c), lambda i, j: (i, 0)),
            pl.BlockSpec((tm, c), lambda i, j: (i, 0)),
            _cols(d, tn, layer, col_ga),
            _cols(d, tn, layer, col_gb),
            pl.BlockSpec((c, tn), lambda i, j: (0, j)),
            pl.BlockSpec((c, tn), lambda i, j: (0, j)),
        ],
        out_specs=pl.BlockSpec((tm, tn), lambda i, j: (i, j)),
        out_shape=jax.ShapeDtypeStruct((t, n), BF16),
        compiler_params=_cparams("parallel", "parallel"),
        name="merge",
    )(h, act, att, w, w, wco, wao)


def _first_argmax(x, m, lane):
    return jnp.min(jnp.where(x == m, lane, float(LANES)), axis=-1, keepdims=True)


def _router(logits):
    lane = lax.broadcasted_iota(jnp.int32, logits.shape, 1).astype(F32)
    neg = -jnp.inf
    lg = jnp.where(lane >= N_EXPERTS, jnp.where(lane < N_EXPERTS + N_GROUPS, logits, neg), neg)
    mg = jnp.max(lg, axis=-1, keepdims=True)
    p_grp = 1.0 / jnp.sum(jnp.exp(lg - mg), axis=-1, keepdims=True)
    g_idx = _first_argmax(lg, mg, lane) - N_EXPERTS
    in_group = jnp.floor(lane * (1.0 / EXPERTS_PER_GROUP)) == g_idx
    le = jnp.where(in_group, logits, neg)
    m1 = jnp.max(le, axis=-1, keepdims=True)
    i1 = _first_argmax(le, m1, lane)
    le2 = jnp.where(lane == i1, neg, le)
    m2 = jnp.max(le2, axis=-1, keepdims=True)
    i2 = _first_argmax(le2, m2, lane)
    e2 = jnp.exp(m2 - m1)
    w1 = p_grp / (1.0 + e2)
    w2 = w1 * e2
    gate = jnp.where(lane == i1, w1, 0.0) + jnp.where(lane == i2, w2, 0.0)
    route = jnp.where(lane == 0, i1, jnp.where(lane == 1, i2, jnp.where(lane == 2, w1, jnp.where(lane == 3, w2, 0.0))))
    return gate, route


def _split_dot(x_hi, x_lo, w_hi_ref, w_lo_ref):
    w_hi = w_hi_ref[...]
    return _dot(x_hi, w_hi) + (_dot(x_lo, w_hi) + _dot(x_hi, w_lo_ref[...]))


def _oproj_kernel(m_ref, wo_ref, x_ref, gt_ref, g2_ref, sc_ref, sh_ref, wrh_ref, wrl_ref, br_ref,
                  x1_ref, h2_ref, gate_ref, route_ref):
    x1 = x_ref[...] + gt_ref[...] * _dot(m_ref[...], wo_ref[...])
    x1_ref[...] = x1
    h2 = _rms_mod(x1, g2_ref[...], sc_ref[...], sh_ref[...])
    hi = h2.astype(BF16)
    h2_ref[...] = hi
    lo = (h2 - hi.astype(F32)).astype(BF16)
    gate_ref[...], route_ref[...] = _router(_split_dot(hi, lo, wrh_ref, wrl_ref) + br_ref[...])


def _pad_lanes(w):
    return jnp.pad(w, ((0, 0), (0, LANES - w.shape[-1])))


def _hi_lo(w):
    hi = w.astype(BF16)
    return hi, (w - hi.astype(F32)).astype(BF16)


def oproj_router(m, wo, x, gt, g2, sc, sh, w_rg, b_rg, w_re, b_re, tm):
    t, d = x.shape
    wrh, wrl = _hi_lo(_pad_lanes(jnp.concatenate([w_re, w_rg], axis=1)))
    br = _pad_lanes(jnp.concatenate([b_re, b_rg]).reshape(1, -1))
    full = lambda shape: pl.BlockSpec(shape, lambda i: (0, 0))
    return pl.pallas_call(
        _oproj_kernel,
        grid=(t // tm,),
        in_specs=[
            pl.BlockSpec((tm, d), lambda i: (i, 0)),
            full((d, d)),
            pl.BlockSpec((tm, d), lambda i: (i, 0)),
            _row_spec(gt.shape[0], tm, d),
            full((1, d)),
            _row_spec(sc.shape[0], tm, d),
            _row_spec(sh.shape[0], tm, d),
            full((d, LANES)), full((d, LANES)), full((1, LANES)),
        ],
        out_specs=[
            pl.BlockSpec((tm, d), lambda i: (i, 0)),
            pl.BlockSpec((tm, d), lambda i: (i, 0)),
            pl.BlockSpec((tm, LANES), lambda i: (i, 0)),
            pl.BlockSpec((tm, LANES), lambda i: (i, 0)),
        ],
        out_shape=[
            jax.ShapeDtypeStruct((t, d), F32),
            jax.ShapeDtypeStruct((t, d), BF16),
            jax.ShapeDtypeStruct((t, LANES), F32),
            jax.ShapeDtypeStruct((t, LANES), F32),
        ],
        compiler_params=_cparams("parallel"),
        name="oproj_router",
    )(m, wo, x, gt, g2.reshape(1, d), sc, sh, wrh, wrl, br)


def _moe_kernel(h_ref, gate_ref, wg_ref, wu_ref, wd_ref, x_ref, gt_ref, o_ref, wg16_ref, wu16_ref, wd16_ref, acc_ref):
    e = pl.program_id(0)

    @pl.when(e == 0)
    def _():
        acc_ref[...] = jnp.zeros_like(acc_ref)

    wg = wg_ref[0].astype(BF16)
    wu = wu_ref[0].astype(BF16)
    wd = wd_ref[0].astype(BF16)
    wg16_ref[0] = wg
    wu16_ref[0] = wu
    wd16_ref[0] = wd
    h = h_ref[...]
    gate = gate_ref[...]
    lane = lax.broadcasted_iota(jnp.int32, gate.shape, 1)
    ge = jnp.sum(jnp.where(lane == e, gate, 0.0), axis=-1, keepdims=True)
    act = (_silu(_dot(h, wg)) * _dot(h, wu)) * ge
    acc_ref[...] += _dot(act.astype(BF16), wd)

    @pl.when(e == pl.num_programs(0) - 1)
    def _():
        o_ref[...] = x_ref[...] + gt_ref[...] * acc_ref[...]


def moe_dense(h2, gate, w_eg, w_eu, w_ed, layer, x1, gt):
    t, d = x1.shape
    _, n_e, _, f = w_eg.shape
    whole = lambda width: pl.BlockSpec((t, width), lambda e: (0, 0))
    up_in = pl.BlockSpec((None, 1, d, f), lambda e: (layer, e, 0, 0))
    down_in = pl.BlockSpec((None, 1, f, d), lambda e: (layer, e, 0, 0))
    up = pl.BlockSpec((1, d, f), lambda e: (e, 0, 0))
    down = pl.BlockSpec((1, f, d), lambda e: (e, 0, 0))
    return pl.pallas_call(
        _moe_kernel,
        grid=(n_e,),
        in_specs=[whole(d), whole(LANES), up_in, up_in, down_in, whole(d),
                  pl.BlockSpec((gt.shape[0], d), lambda e: (0, 0))],
        out_specs=[whole(d), up, up, down],
        out_shape=[
            jax.ShapeDtypeStruct((t, d), F32),
            jax.ShapeDtypeStruct((n_e, d, f), BF16),
            jax.ShapeDtypeStruct((n_e, d, f), BF16),
            jax.ShapeDtypeStruct((n_e, f, d), BF16),
        ],
        scratch_shapes=[pltpu.VMEM((t, d), F32)],
        compiler_params=_cparams("arbitrary"),
        name="moe_dense",
    )(h2, gate, w_eg, w_eu, w_ed, x1, gt)


SEG_ALIGN = BF16_ROWS
EXPERT_ROWS = 128
STEP_EXPERTS = 2
SORT_ROWS = MXU_DIM


ROUTE_ROWS = 8


def _route_pos_kernel(route_ref, col_ref, row_ref, seg_ref):
    route = route_ref[...]
    tm = route.shape[0]
    lane = lax.broadcasted_iota(jnp.int32, route.shape, 1).astype(F32)
    oh1 = lane == route[:, 0:1]
    oh2 = lane == route[:, 1:2]
    both = jnp.where(oh1, 1.0, jnp.where(oh2, 1.0, 0.0))
    before = (lax.broadcasted_iota(jnp.int32, (tm, tm), 1) < lax.broadcasted_iota(jnp.int32, (tm, tm), 0))
    earlier = _dot(jnp.where(before, 1.0, 0.0).astype(BF16), both.astype(BF16))
    units = jnp.floor((jnp.sum(both, axis=0, keepdims=True) + (SEG_ALIGN - 1)) * (1.0 / SEG_ALIGN))
    lower = (lax.broadcasted_iota(jnp.int32, (LANES, LANES), 0) < lax.broadcasted_iota(jnp.int32, (LANES, LANES), 1))
    start = SEG_ALIGN * _dot(jnp.broadcast_to(units, (ROUTE_ROWS, LANES)).astype(BF16),
                             jnp.where(lower, 1.0, 0.0).astype(BF16))[0:1]
    size = SEG_ALIGN * units
    spot = earlier + start
    pos1 = jnp.sum(jnp.where(oh1, spot, 0.0), axis=-1, keepdims=True)
    pos2 = jnp.sum(jnp.where(oh2, spot, 0.0), axis=-1, keepdims=True)
    col = jnp.where(lane == 0, pos1, jnp.where(lane == 1, pos2, route))
    col_ref[...] = col
    row_ref[0] = jnp.transpose(col)[0:ROUTE_ROWS, :]
    sub = lax.broadcasted_iota(jnp.int32, (ROUTE_ROWS, LANES), 0)
    trips = jnp.floor((size + (EXPERT_ROWS - 1)) * (1.0 / EXPERT_ROWS))
    seg_ref[0] = jnp.where(sub == 0, start, jnp.where(sub == 1, start + size, jnp.where(sub == 2, trips, 0.0)))


def route_pos(route, tm):
    t = route.shape[0]
    nt = t // tm
    col, row, seg = pl.pallas_call(
        _route_pos_kernel,
        grid=(nt,),
        in_specs=[pl.BlockSpec((tm, LANES), lambda i: (i, 0))],
        out_specs=[
            pl.BlockSpec((tm, LANES), lambda i: (i, 0)),
            pl.BlockSpec((1, ROUTE_ROWS, tm), lambda i: (i, 0, 0)),
            pl.BlockSpec((1, ROUTE_ROWS, LANES), lambda i: (i, 0, 0)),
        ],
        out_shape=[
            jax.ShapeDtypeStruct((t, LANES), F32),
            jax.ShapeDtypeStruct((nt, ROUTE_ROWS, tm), F32),
            jax.ShapeDtypeStruct((nt, ROUTE_ROWS, LANES), F32),
        ],
        compiler_params=_cparams("parallel"),
        name="route_pos",
    )(route)
    seg = seg[:, 0:3, 0:N_EXPERTS].astype(jnp.int32)
    return col, row, seg[:, 0].reshape(-1), seg[:, 1].reshape(-1), seg[:, 2].reshape(-1)


def _moe_routed_kernel(start_ref, end_ref, trips_ref, h_ref, col_ref, row_ref, wg_ref, wu_ref, wd_ref, o_ref,
                       xy_ref, wrow_ref):
    i = pl.program_id(0)
    e = pl.program_id(1)
    n_e = pl.num_programs(1)
    tm = h_ref.shape[0]
    rows = xy_ref.shape[0]

    @pl.when(e == 0)
    def _():
        pos1, pos2 = row_ref[0, 0:1, :], row_ref[0, 1:2, :]
        w1, w2 = row_ref[0, 2:3, :], row_ref[0, 3:4, :]

        def sort(c, carry):
            r0 = pl.multiple_of(c * SORT_ROWS, SORT_ROWS)
            row = (r0 + lax.broadcasted_iota(jnp.int32, (SORT_ROWS, tm), 0)).astype(F32)
            hit1 = pos1 == row
            hit2 = pos2 == row
            pick = jnp.where(hit1, 1.0, jnp.where(hit2, 1.0, 0.0)).astype(BF16)
            xy_ref[pl.ds(r0, SORT_ROWS), :] = _dot(pick, h_ref[...]).astype(BF16)
            wrow_ref[pl.ds(r0, SORT_ROWS), :] = jnp.sum(
                jnp.where(hit1, w1, jnp.where(hit2, w2, 0.0)), axis=-1, keepdims=True)
            return carry
        lax.fori_loop(0, rows // SORT_ROWS, sort, 0)

    for k in range(STEP_EXPERTS):
        seg = (i * n_e + e) * STEP_EXPERTS + k
        seg_start = start_ref[seg]
        seg_end = end_ref[seg]

        def expert(c, carry, k=k, seg_start=seg_start, seg_end=seg_end):
            r0 = pl.multiple_of(seg_start + c * EXPERT_ROWS, SEG_ALIGN)
            xs = xy_ref[pl.ds(r0, EXPERT_ROWS), :]
            w_row = wrow_ref[pl.ds(r0, EXPERT_ROWS), :]
            act = (_silu(_dot(xs, wg_ref[k])) * _dot(xs, wu_ref[k])) * w_row
            y = _dot(act.astype(BF16), wd_ref[k]).astype(BF16)
            row = r0 + lax.broadcasted_iota(jnp.int32, (EXPERT_ROWS, 1), 0)
            xy_ref[pl.ds(r0, EXPERT_ROWS), :] = jnp.where(row < seg_end, y, xs)
            return carry

        lax.fori_loop(0, trips_ref[seg], expert, 0)

    @pl.when(e == n_e - 1)
    def _():
        def combine(c, carry):
            t0 = pl.multiple_of(c * SORT_ROWS, SORT_ROWS)
            row = lax.broadcasted_iota(jnp.int32, (SORT_ROWS, rows), 1).astype(F32)
            hit1 = col_ref[pl.ds(t0, SORT_ROWS), 0:1] == row
            hit2 = col_ref[pl.ds(t0, SORT_ROWS), 1:2] == row
            pick = jnp.where(hit1, 1.0, jnp.where(hit2, 1.0, 0.0)).astype(BF16)
            o_ref[pl.ds(t0, SORT_ROWS), :] = _dot(pick, xy_ref[...])
            return carry
        lax.fori_loop(0, tm // SORT_ROWS, combine, 0)


def moe_routed(h2, route, w_eg, w_eu, w_ed, tm):
    t, d = h2.shape
    n_e, _, f = w_eg.shape
    rows = _round_up(2 * tm + n_e * (SEG_ALIGN - 1) + EXPERT_ROWS, SORT_ROWS)
    col, row, start, end, trips = route_pos(route, tm)
    grid_spec = pltpu.PrefetchScalarGridSpec(
        num_scalar_prefetch=3,
        grid=(t // tm, n_e // STEP_EXPERTS),
        in_specs=[
            pl.BlockSpec((tm, d), lambda i, e, *_: (i, 0)),
            pl.BlockSpec((tm, LANES), lambda i, e, *_: (i, 0)),
            pl.BlockSpec((1, ROUTE_ROWS, tm), lambda i, e, *_: (i, 0, 0)),
            pl.BlockSpec((STEP_EXPERTS, d, f), lambda i, e, *_: (e, 0, 0)),
            pl.BlockSpec((STEP_EXPERTS, d, f), lambda i, e, *_: (e, 0, 0)),
            pl.BlockSpec((STEP_EXPERTS, f, d), lambda i, e, *_: (e, 0, 0)),
        ],
        out_specs=pl.BlockSpec((tm, d), lambda i, e, *_: (i, 0)),
        scratch_shapes=[pltpu.VMEM((rows, d), BF16), pltpu.VMEM((rows, 1), F32)],
    )
    return pl.pallas_call(
        _moe_routed_kernel,
        grid_spec=grid_spec,
        out_shape=jax.ShapeDtypeStruct((t, d), F32),
        compiler_params=_cparams("parallel", "arbitrary", vmem=VMEM_LIMIT_MOE),
        name="moe_routed",
    )(start, end, trips, h2, col, row, w_eg, w_eu, w_ed)


def _mix_rows(x, h, mods, wts, layer, tm, k_all, v_all, conv_fn, attn_fn):
    _, _, gt1, sh2, sc2, _ = mods
    d = x.shape[1]
    w_in = wts["w_in"]
    c_ch = wts["w_co"].shape[0]
    attn_w = wts["w_ao"].shape[0]
    col_q = 2 * c_ch
    col_g = col_q + 3 * attn_w
    u = glu_proj(h, w_in, layer, 0, c_ch, c_ch, tm)
    q, k_all, k16, v_all, v16 = qkv_proj(h, w_in, layer, col_q, attn_w, k_all, v_all, tm)
    act, new_conv = conv_fn(u)
    att = attn_fn(q, k16, v16)
    m = merge(h, act, att, w_in, layer, col_g, col_g + d, wts["w_co"], wts["w_ao"], tm)
    x1, h2, gate, route = oproj_router(m, wts["w_o"], x, gt1, wts["g_n2"], sc2, sh2,
                                       wts["w_rg"], wts["b_rg"], wts["w_re"], wts["b_re"], min(tm, 256))
    return x1, h2, gate, route, k_all, v_all, new_conv


def kernel(x_prompt, x_sample, cache_k, cache_v, state_conv, page_table, c_prompt, c_sample, w_ada, b_ada, g_n1, w_in, b_sb, w_dw, b_dw, ln_g, ln_b, w_conv_out, w_attn_out, w_o, g_n2, w_rg, b_rg, w_re, b_re, w_eg, w_eu, w_ed, g_final):
    batch, seq, d = x_prompt.shape
    n_seq, t_new, _ = x_sample.shape
    assert batch == 1 and t_new == 1, "one prompt sequence and one new token per sample sequence"
    n_layers = w_in.shape[0]
    c_ch = w_dw.shape[-1]
    attn_w = N_HEADS * HEAD_DIM
    hist = CONV_KERNEL - 1
    assert seq >= hist

    mod_rows = n_seq + BF16_ROWS
    c_all = jnp.zeros((mod_rows, d), F32).at[:n_seq].set(c_sample).at[n_seq:n_seq + 1].set(c_prompt)
    mod = ada_mod(c_all, w_ada, b_ada)

    tm_p = min(1024, seq)
    tm_n = min(512, seq)
    xp = x_prompt.reshape(seq, d)
    xs = x_sample.reshape(n_seq, d)
    pending = None
    zero_halo = jnp.zeros((HALO, c_ch), F32)
    zeros_row = jnp.zeros((1, d), F32)
    kp = jnp.zeros((n_layers, seq, attn_w), F32)
    vp = jnp.zeros((n_layers, seq, attn_w), F32)
    ks = jnp.zeros((n_layers, n_seq, attn_w), F32)
    vs = jnp.zeros((n_layers, n_seq, attn_w), F32)
    cs = jnp.zeros(state_conv.shape, F32)
    cp = []
    w_in16 = w_in.astype(BF16)
    for l in range(n_layers):
        wts = dict(
            g_n2=g_n2[l], w_in=w_in16,
            w_co=w_conv_out[l].astype(BF16), w_ao=w_attn_out[l].astype(BF16), w_o=w_o[l].astype(BF16),
            w_rg=w_rg[l], b_rg=b_rg[l], w_re=w_re[l], b_re=b_re[l],
        )
        mods_s = tuple(mod[l, :n_seq, i * d:(i + 1) * d] for i in range(6))
        mods_p = tuple(mod[l, n_seq:n_seq + 1, i * d:(i + 1) * d] for i in range(6))

        def conv_p(u, l=l):
            act = conv_prompt(u, zero_halo, w_dw[l], b_dw[l], ln_g[l], ln_b[l])
            return act, u[seq - hist:]

        def attn_p(q, k16, v16, l=l):
            return attn_prompt(q, k16, v16, b_sb[l])

        def conv_s(u, l=l, cs=cs):
            return conv_sample(u, state_conv, l, w_dw[l], b_dw[l], ln_g[l], ln_b[l], cs)

        def attn_s(q, k16, v16, l=l):
            att = attn_sample(q.reshape(n_seq, N_HEADS, HEAD_DIM), cache_k, cache_v, l, page_table, b_sb[l])
            return att.reshape(n_seq, attn_w)

        hs = norm_mod(xs, g_n1[l], mods_s[1], mods_s[0], BF16, n_seq)
        x1s, h2s, gate_s, _, ks, vs, cs = _mix_rows(xs, hs, mods_s, wts, l, n_seq, ks, vs, conv_s, attn_s)
        xs, weg, weu, wed = moe_dense(h2s, gate_s, w_eg, w_eu, w_ed, l, x1s, mods_s[5])

        if pending is None:
            hp = norm_mod(xp, g_n1[l], mods_p[1], mods_p[0], BF16, tm_p)
        else:
            xp, hp = resid_norm(*pending, g_n1[l], mods_p[1], mods_p[0], BF16, tm_n)
        x1, h2, _, route, kp, vp, cp_l = _mix_rows(xp, hp, mods_p, wts, l, tm_p, kp, vp, conv_p, attn_p)
        pending = (x1, moe_routed(h2, route, weg, weu, wed, tm_p), mods_p[5])
        cp.append(cp_l.reshape(1, hist, c_ch))

    _, y_prompt = resid_norm(*pending, g_final, zeros_row, zeros_row, F32, tm_n)
    y_sample = norm_mod(xs, g_final, zeros_row, zeros_row, F32, n_seq)
    return (y_prompt.reshape(1, seq, d), y_sample.reshape(n_seq, 1, d),
            kp.reshape(n_layers, 1, seq, N_HEADS, HEAD_DIM), vp.reshape(n_layers, 1, seq, N_HEADS, HEAD_DIM),
            jnp.stack(cp),
            ks.reshape(n_layers, n_seq, 1, N_HEADS, HEAD_DIM), vs.reshape(n_layers, n_seq, 1, N_HEADS, HEAD_DIM), cs)
```

```python
import functools

import jax
import jax.numpy as jnp
from jax import lax
from jax.experimental import pallas as pl
from jax.experimental.pallas import tpu as pltpu

F32 = jnp.float32
BF16 = jnp.bfloat16

N_HEADS = 8
HEAD_DIM = 128
CONV_KERNEL = 31
N_GROUPS = 4
EXPERTS_PER_GROUP = 8
N_EXPERTS = N_GROUPS * EXPERTS_PER_GROUP
EPS = 1e-6

LOG2E = 1.4426950408889634
Q_SCALE = HEAD_DIM ** -0.5 * LOG2E

LANES = 128
SUBLANES = 8
BF16_ROWS = 16
MXU_DIM = 256
V7X_VMEM_BYTES = 64 * 1024 * 1024
VMEM_LIMIT = V7X_VMEM_BYTES - 8 * 1024 * 1024
VMEM_LIMIT_MOE = V7X_VMEM_BYTES - 4 * 1024 * 1024


def _cparams(*sem, vmem=VMEM_LIMIT):
    return pltpu.CompilerParams(dimension_semantics=sem, vmem_limit_bytes=vmem)


def _dot(a, b):
    return jnp.dot(a, b, preferred_element_type=F32)


def _dot_nt(a, b):
    return lax.dot_general(a, b, (((1,), (1,)), ((), ())), preferred_element_type=F32)


def _silu(x):
    return x * jax.nn.sigmoid(x)


def _round_up(x, m):
    return (x + m - 1) // m * m


def _row_spec(rows, tm, width):
    if rows == 1:
        return pl.BlockSpec((1, width), lambda i, *_: (0, 0))
    return pl.BlockSpec((tm, width), lambda i, *_: (i, 0))


def _ada_kernel(c_ref, w_ref, b_ref, o_ref):
    a = _silu(c_ref[...]).astype(BF16)
    o_ref[0] = _dot(a, w_ref[0].astype(BF16)) + b_ref[0]


def ada_mod(c_all, w_ada, b_ada, tn=512):
    n_layers, d, n = w_ada.shape
    rows = c_all.shape[0]
    return pl.pallas_call(
        _ada_kernel,
        grid=(n_layers, n // tn),
        in_specs=[
            pl.BlockSpec((rows, d), lambda l, j: (0, 0)),
            pl.BlockSpec((1, d, tn), lambda l, j: (l, 0, j)),
            pl.BlockSpec((1, 1, tn), lambda l, j: (l, 0, j)),
        ],
        out_specs=pl.BlockSpec((1, rows, tn), lambda l, j: (l, 0, j)),
        out_shape=jax.ShapeDtypeStruct((n_layers, rows, n), F32),
        compiler_params=_cparams("parallel", "parallel"),
        name="ada_mod",
    )(c_all, w_ada, b_ada.reshape(n_layers, 1, n))


def _rms_mod(x, g, sc, sh):
    y = x * lax.rsqrt(jnp.mean(x * x, axis=-1, keepdims=True) + EPS)
    return (y * g) * (1.0 + sc) + sh


def _norm_mod_kernel(x_ref, g_ref, sc_ref, sh_ref, o_ref):
    o_ref[...] = _rms_mod(x_ref[...], g_ref[...], sc_ref[...], sh_ref[...]).astype(o_ref.dtype)


def norm_mod(x, g, sc, sh, out_dtype, tm):
    t, d = x.shape
    return pl.pallas_call(
        _norm_mod_kernel,
        grid=(t // tm,),
        in_specs=[
            pl.BlockSpec((tm, d), lambda i: (i, 0)),
            pl.BlockSpec((1, d), lambda i: (0, 0)),
            _row_spec(sc.shape[0], tm, d),
            _row_spec(sh.shape[0], tm, d),
        ],
        out_specs=pl.BlockSpec((tm, d), lambda i: (i, 0)),
        out_shape=jax.ShapeDtypeStruct((t, d), out_dtype),
        compiler_params=_cparams("parallel"),
        name="norm_mod",
    )(x, g.reshape(1, d), sc, sh)


def _resid_norm_kernel(x_ref, m_ref, gt_ref, g_ref, sc_ref, sh_ref, x2_ref, o_ref):
    x = x_ref[...] + gt_ref[...] * m_ref[...]
    x2_ref[...] = x
    o_ref[...] = _rms_mod(x, g_ref[...], sc_ref[...], sh_ref[...]).astype(o_ref.dtype)


def resid_norm(x, m, gt, g, sc, sh, out_dtype, tm):
    t, d = x.shape
    return pl.pallas_call(
        _resid_norm_kernel,
        grid=(t // tm,),
        in_specs=[
            pl.BlockSpec((tm, d), lambda i: (i, 0)),
            pl.BlockSpec((tm, d), lambda i: (i, 0)),
            _row_spec(gt.shape[0], tm, d),
            pl.BlockSpec((1, d), lambda i: (0, 0)),
            _row_spec(sc.shape[0], tm, d),
            _row_spec(sh.shape[0], tm, d),
        ],
        out_specs=[pl.BlockSpec((tm, d), lambda i: (i, 0)), pl.BlockSpec((tm, d), lambda i: (i, 0))],
        out_shape=[jax.ShapeDtypeStruct((t, d), F32), jax.ShapeDtypeStruct((t, d), out_dtype)],
        compiler_params=_cparams("parallel"),
        name="resid_norm",
    )(x, m, gt, g.reshape(1, d), sc, sh)


def _glu_kernel(h_ref, wa_ref, wb_ref, u_ref):
    h = h_ref[...]
    u_ref[...] = _dot(h, wa_ref[...]) * jax.nn.sigmoid(_dot(h, wb_ref[...]))


def _cols(d, tn, layer, col0):
    first = col0 // tn
    assert first * tn == col0
    return pl.BlockSpec((None, d, tn), lambda i, j: (layer, 0, first + j))


def glu_proj(h, w, layer, col_a, col_b, n, tm, tn=512):
    t, d = h.shape
    return pl.pallas_call(
        _glu_kernel,
        grid=(t // tm, n // tn),
        in_specs=[pl.BlockSpec((tm, d), lambda i, j: (i, 0)),
                  _cols(d, tn, layer, col_a), _cols(d, tn, layer, col_b)],
        out_specs=pl.BlockSpec((tm, tn), lambda i, j: (i, j)),
        out_shape=jax.ShapeDtypeStruct((t, n), F32),
        compiler_params=_cparams("parallel", "parallel"),
        name="glu_proj",
    )(h, w, w)


def _qkv_kernel(h_ref, wq_ref, wk_ref, wv_ref, k_all_ref, v_all_ref, q_ref, k32_ref, k16_ref, v32_ref, v16_ref):
    del k_all_ref, v_all_ref
    h = h_ref[...]
    q_ref[...] = (_dot(h, wq_ref[...]) * Q_SCALE).astype(q_ref.dtype)
    k = _dot(h, wk_ref[...])
    k32_ref[...] = k
    k16_ref[...] = k.astype(k16_ref.dtype)
    v = _dot(h, wv_ref[...])
    v32_ref[...] = v
    v16_ref[...] = v.astype(v16_ref.dtype)


def qkv_proj(h, w, layer, col_q, n, k_all, v_all, tm, tn=512):
    t, d = h.shape
    out = pl.BlockSpec((tm, tn), lambda i, j: (i, j))
    out_l = pl.BlockSpec((None, tm, tn), lambda i, j: (layer, i, j))
    stacked = jax.ShapeDtypeStruct(k_all.shape, F32)
    flat = lambda dt: jax.ShapeDtypeStruct((t, n), dt)
    return pl.pallas_call(
        _qkv_kernel,
        grid=(t // tm, n // tn),
        in_specs=[pl.BlockSpec((tm, d), lambda i, j: (i, 0)),
                  _cols(d, tn, layer, col_q), _cols(d, tn, layer, col_q + n), _cols(d, tn, layer, col_q + 2 * n),
                  pl.BlockSpec(memory_space=pl.ANY), pl.BlockSpec(memory_space=pl.ANY)],
        out_specs=[out, out_l, out, out_l, out],
        out_shape=[flat(BF16), stacked, flat(BF16), stacked, flat(BF16)],
        input_output_aliases={4: 1, 5: 3},
        compiler_params=_cparams("parallel", "parallel"),
        name="qkv_proj",
    )(h, w, w, w, k_all, v_all)


def _layer_norm_silu(dc, g, b):
    mu = jnp.mean(dc, axis=-1, keepdims=True)
    xc = dc - mu
    var = jnp.mean(xc * xc, axis=-1, keepdims=True)
    return _silu((xc * lax.rsqrt(var + EPS)) * g + b)


HALO = 32
CONV_ROWS = 128


def _conv_prompt_kernel(u_ref, halo_ref, prev_ref, w_ref, b_ref, g_ref, beta_ref, o_ref, buf_ref, dc_ref, win_ref):
    i = pl.program_id(0)
    tt, c = u_ref.shape

    @pl.when(i == 0)
    def _():
        buf_ref[0:HALO, :] = prev_ref[...]

    @pl.when(i > 0)
    def _():
        buf_ref[0:HALO, :] = halo_ref[...]

    buf_ref[HALO:HALO + tt, :] = u_ref[...]
    first = HALO - (CONV_KERNEL - 1)
    by_phase = {}
    for j in range(CONV_KERNEL):
        by_phase.setdefault((first + j) % SUBLANES, []).append(first + j)
    rows = min(tt, CONV_ROWS)

    def lane_chunk(cc, carry):
        ls = pl.ds(pl.multiple_of(cc * LANES, LANES), LANES)
        for r0 in range(0, tt, rows):
            acc = jnp.broadcast_to(b_ref[:, ls], (rows, LANES))
            for offsets in by_phase.values():
                lo, hi = offsets[0], offsets[-1]
                win_ref[0:hi - lo + rows, :] = buf_ref[r0 + lo:r0 + hi + rows, ls]
                for o in offsets:
                    j = o - first
                    acc = acc + w_ref[j:j + 1, ls] * win_ref[o - lo:o - lo + rows, :]
            dc_ref[r0:r0 + rows, ls] = acc
        return carry

    lax.fori_loop(0, c // LANES, lane_chunk, 0)
    o_ref[...] = _layer_norm_silu(dc_ref[...], g_ref[...], beta_ref[...]).astype(o_ref.dtype)


def conv_prompt(u, prev_halo, w_dw, b_dw, ln_g, ln_b, tt=256):
    t, c = u.shape
    ratio = tt // HALO
    return pl.pallas_call(
        _conv_prompt_kernel,
        grid=(t // tt,),
        in_specs=[
            pl.BlockSpec((tt, c), lambda i: (i, 0)),
            pl.BlockSpec((HALO, c), lambda i: (jnp.maximum(i * ratio - 1, 0), 0)),
            pl.BlockSpec((HALO, c), lambda i: (0, 0)),
            pl.BlockSpec((CONV_KERNEL, c), lambda i: (0, 0)),
            pl.BlockSpec((1, c), lambda i: (0, 0)),
            pl.BlockSpec((1, c), lambda i: (0, 0)),
            pl.BlockSpec((1, c), lambda i: (0, 0)),
        ],
        out_specs=pl.BlockSpec((tt, c), lambda i: (i, 0)),
        out_shape=jax.ShapeDtypeStruct((t, c), BF16),
        scratch_shapes=[pltpu.VMEM((HALO + tt, c), F32), pltpu.VMEM((tt, c), F32),
                        pltpu.VMEM((HALO + min(tt, CONV_ROWS), LANES), F32)],
        compiler_params=_cparams("arbitrary"),
        name="conv_prompt",
    )(u, u, prev_halo, w_dw, b_dw.reshape(1, c), ln_g.reshape(1, c), ln_b.reshape(1, c))


def _conv_sample_kernel(u_ref, prev_ref, w_ref, b_ref, g_ref, beta_ref, new_all_ref, act_ref, new_ref, dc_ref):
    del new_all_ref
    bb = u_ref.shape[0]
    hist = CONV_KERNEL - 1
    w_hist = w_ref[0:hist, :]
    w_last = w_ref[hist:hist + 1, :]
    for b in range(bb):
        prev = prev_ref[b]
        u_row = u_ref[b:b + 1, :]
        dc_ref[b:b + 1, :] = (jnp.sum(prev * w_hist, axis=0, keepdims=True) + w_last * u_row + b_ref[...])
        new_ref[b, 0:hist - 1, :] = prev_ref[b, 1:hist, :]
        new_ref[b, hist - 1:hist, :] = u_row
    act_ref[...] = _layer_norm_silu(dc_ref[...], g_ref[...], beta_ref[...]).astype(act_ref.dtype)


def conv_sample(u, state, layer, w_dw, b_dw, ln_g, ln_b, new_all, bb=8):
    n, c = u.shape
    hist = state.shape[2]
    hist_spec = pl.BlockSpec((None, bb, hist, c), lambda i: (layer, i, 0, 0))
    return pl.pallas_call(
        _conv_sample_kernel,
        grid=(n // bb,),
        in_specs=[
            pl.BlockSpec((bb, c), lambda i: (i, 0)),
            hist_spec,
            pl.BlockSpec((CONV_KERNEL, c), lambda i: (0, 0)),
            pl.BlockSpec((1, c), lambda i: (0, 0)),
            pl.BlockSpec((1, c), lambda i: (0, 0)),
            pl.BlockSpec((1, c), lambda i: (0, 0)),
            pl.BlockSpec(memory_space=pl.ANY),
        ],
        out_specs=[pl.BlockSpec((bb, c), lambda i: (i, 0)), hist_spec],
        out_shape=[jax.ShapeDtypeStruct((n, c), BF16), jax.ShapeDtypeStruct(new_all.shape, F32)],
        input_output_aliases={6: 1},
        scratch_shapes=[pltpu.VMEM((bb, c), F32)],
        compiler_params=_cparams("parallel"),
        name="conv_sample",
    )(u, state, w_dw, b_dw.reshape(1, c), ln_g.reshape(1, c), ln_b.reshape(1, c), new_all)


SOFTPLUS_CAP = 64.0


def _softplus2(z2):
    return jnp.maximum(z2, jnp.log(1.0 + jnp.exp2(jnp.minimum(z2, SOFTPLUS_CAP))) * LOG2E)


def _strict_lower(n):
    return (lax.broadcasted_iota(jnp.int32, (n, n), 0) > lax.broadcasted_iota(jnp.int32, (n, n), 1)).astype(BF16)


def _attn_prompt_kernel(bias_ref, q_ref, k_ref, v_ref, o_ref, acc_ref, carry_ref, *, tk, heads):
    qi = pl.program_id(1)
    tq = q_ref.shape[0]
    n_diag = tq // tk
    upper = _strict_lower(tk)
    acc_ref[...] = jnp.zeros_like(acc_ref)
    carry_ref[...] = jnp.zeros_like(carry_ref)

    def tile(hh, kb, masked):
        cols = slice(hh * HEAD_DIM, (hh + 1) * HEAD_DIM)
        start = pl.multiple_of(kb * tk, tk)
        ks = k_ref[pl.ds(start, tk), cols]
        vs = v_ref[pl.ds(start, tk), cols]
        z = _dot_nt(q_ref[:, cols], ks) + bias_ref[hh]
        sp = _softplus2(z)
        ls = z - sp
        if masked:
            t_pos = qi * tq + lax.broadcasted_iota(jnp.int32, (tq, tk), 0)
            s_pos = kb * tk + lax.broadcasted_iota(jnp.int32, (tq, tk), 1)
            valid = s_pos < t_pos
            sp = jnp.where(valid, sp, 0.0)
        spb = sp.astype(BF16)
        later = _dot(spb, upper)
        carry = carry_ref[hh]
        a = jnp.exp2(ls - later)
        if masked:
            a = jnp.where(valid, a, 0.0)
        acc_ref[hh] += _dot(a.astype(BF16), vs) * jnp.exp2(-carry)
        carry_ref[hh] = carry + later[:, 0:1] + spb[:, 0:1].astype(F32)

    for d in range(n_diag):
        for hh in range(heads):
            tile(hh, qi * n_diag + (n_diag - 1 - d), True)

    def body(j, c):
        for d in range(n_diag):
            for hh in range(heads):
                tile(hh, (qi - j) * n_diag - 1 - d, False)
        return c

    lax.fori_loop(0, qi, body, 0)
    for hh in range(heads):
        o_ref[:, hh * HEAD_DIM:(hh + 1) * HEAD_DIM] = acc_ref[hh].astype(o_ref.dtype)


def attn_prompt(q, k, v, b_sb, tq=512, tk=MXU_DIM, heads=4):
    t = q.shape[0]
    tq = min(tq, t)
    width = heads * HEAD_DIM
    bias = jnp.broadcast_to((b_sb.astype(F32) * LOG2E)[:, None, None], (N_HEADS, 1, tk))
    kern = functools.partial(_attn_prompt_kernel, tk=tk, heads=heads)
    return pl.pallas_call(
        kern,
        grid=(N_HEADS // heads, t // tq),
        in_specs=[
            pl.BlockSpec((heads, 1, tk), lambda g, i: (g, 0, 0)),
            pl.BlockSpec((tq, width), lambda g, i: (i, g)),
            pl.BlockSpec((t, width), lambda g, i: (0, g)),
            pl.BlockSpec((t, width), lambda g, i: (0, g)),
        ],
        out_specs=pl.BlockSpec((tq, width), lambda g, i: (i, g)),
        out_shape=jax.ShapeDtypeStruct(q.shape, BF16),
        scratch_shapes=[pltpu.VMEM((heads, tq, HEAD_DIM), F32), pltpu.VMEM((heads, tq, 1), F32)],
        compiler_params=_cparams("parallel", "arbitrary"),
        name="attn_prompt",
    )(bias, q, k, v)


def _attn_sample_kernel(pt_ref, bias_ref, q_ref, *refs, pp, page):
    k_refs, v_refs = refs[:pp], refs[pp:2 * pp]
    o_ref, acc_ref, carry_ref = refs[2 * pp:]
    j = pl.program_id(1)
    nh = N_HEADS

    @pl.when(j == 0)
    def _():
        acc_ref[...] = jnp.zeros_like(acc_ref)
        carry_ref[...] = jnp.zeros_like(carry_ref)

    q = q_ref[0]
    head = lax.broadcasted_iota(jnp.int32, (nh, LANES), 0)

    def head_rows(page_refs, h):
        return jnp.concatenate(
            [r[0, 0, pl.ds(h, page, stride=nh), :] for r in page_refs], axis=0).astype(BF16)

    z_pages = [jnp.zeros((nh, page), F32) for _ in range(pp)]
    for h in range(nh):
        z_h = _dot_nt(q, head_rows(k_refs, h))
        z_pages = [jnp.where(head == h, z_h[:, p * page:(p + 1) * page], z_pages[p]) for p in range(pp)]
    z = jnp.concatenate(z_pages, axis=0) + bias_ref[...]
    sp = _softplus2(z)
    spb = sp.astype(BF16)
    later = _dot(spb, _strict_lower(page))
    total = later[:, 0:1] + spb[:, 0:1].astype(F32)
    carry = carry_ref[...]
    carries = [None] * pp
    for p in reversed(range(pp)):
        carries[p] = carry
        carry = carry + total[p * nh:(p + 1) * nh]
    carry_ref[...] = carry
    a = jnp.exp2((z - sp) - later - jnp.concatenate(carries, axis=0))
    a_cat = jnp.concatenate([a[p * nh:(p + 1) * nh] for p in range(pp)], axis=1).astype(BF16)
    acc = acc_ref[...]
    for h in range(nh):
        acc = jnp.where(head == h, acc + _dot(a_cat, head_rows(v_refs, h)), acc)
    acc_ref[...] = acc

    @pl.when(j == pl.num_programs(1) - 1)
    def _():
        o_ref[0] = acc.astype(o_ref.dtype)


def attn_sample(q, cache_k, cache_v, layer, page_table, b_sb, pp=16):
    n_seq, n_pages = page_table.shape
    n_layers, n_pool, page = cache_k.shape[:3]
    pp = min(pp, n_pages)
    steps = n_pages // pp
    ck = cache_k.reshape(n_layers, n_pool, page * N_HEADS, HEAD_DIM)
    cv = cache_v.reshape(n_layers, n_pool, page * N_HEADS, HEAD_DIM)
    bias = jnp.broadcast_to((b_sb.astype(F32) * LOG2E)[None, :, None], (pp, N_HEADS, page)).reshape(pp * N_HEADS, page)

    def page_spec(p):
        def index(b, j, pt):
            return (layer, pt[b * n_pages + (steps - 1 - j) * pp + p], 0, 0)
        return pl.BlockSpec((1, 1, page * N_HEADS, HEAD_DIM), index)

    kern = functools.partial(_attn_sample_kernel, pp=pp, page=page)
    grid_spec = pltpu.PrefetchScalarGridSpec(
        num_scalar_prefetch=1,
        grid=(n_seq, steps),
        in_specs=[
            pl.BlockSpec((pp * N_HEADS, page), lambda b, j, pt: (0, 0)),
            pl.BlockSpec((1, N_HEADS, HEAD_DIM), lambda b, j, pt: (b, 0, 0)),
            *[page_spec(p) for p in range(pp)],
            *[page_spec(p) for p in range(pp)],
        ],
        out_specs=pl.BlockSpec((1, N_HEADS, HEAD_DIM), lambda b, j, pt: (b, 0, 0)),
        scratch_shapes=[pltpu.VMEM((N_HEADS, HEAD_DIM), F32), pltpu.VMEM((N_HEADS, 1), F32)],
    )
    return pl.pallas_call(
        kern,
        grid_spec=grid_spec,
        out_shape=jax.ShapeDtypeStruct((n_seq, N_HEADS, HEAD_DIM), BF16),
        compiler_params=_cparams("parallel", "arbitrary"),
        name="attn_sample",
    )(page_table.reshape(-1), bias, q, *([ck] * pp), *([cv] * pp))


def _merge_kernel(h_ref, act_ref, att_ref, wga_ref, wgb_ref, wco_ref, wao_ref, o_ref):
    h = h_ref[...]
    conv_out = _dot(act_ref[...], wco_ref[...])
    att_out = _dot(att_ref[...], wao_ref[...])
    ga = jax.nn.sigmoid(_dot(h, wga_ref[...]))
    gb = jax.nn.sigmoid(_dot(h, wgb_ref[...]))
    o_ref[...] = (ga * conv_out + gb * att_out).astype(o_ref.dtype)


def merge(h, act, att, w, layer, col_ga, col_gb, wco, wao, tm, tn=512):
    t, d = h.shape
    c = act.shape[1]
    n = wco.shape[1]
    return pl.pallas_call(
        _merge_kernel,
        grid=(t // tm, n // tn),
        in_specs=[
            pl.BlockSpec((tm, d), lambda i, j: (i, 0)),
            pl.BlockSpec((tm, c), lambda i, j: (i, 0)),
            pl.BlockSpec((tm, c), lambda i, j: (i, 0)),
            _cols(d, tn, layer, col_ga),
            _cols(d, tn, layer, col_gb),
            pl.BlockSpec((c, tn), lambda i, j: (0, j)),
            pl.BlockSpec((c, tn), lambda i, j: (0, j)),
        ],
        out_specs=pl.BlockSpec((tm, tn), lambda i, j: (i, j)),
        out_shape=jax.ShapeDtypeStruct((t, n), BF16),
        compiler_params=_cparams("parallel", "parallel"),
        name="merge",
    )(h, act, att, w, w, wco, wao)


def _first_argmax(x, m, lane):
    return jnp.min(jnp.where(x == m, lane, float(LANES)), axis=-1, keepdims=True)


def _router(logits):
    lane = lax.broadcasted_iota(jnp.int32, logits.shape, 1).astype(F32)
    neg = -jnp.inf
    lg = jnp.where(lane >= N_EXPERTS, jnp.where(lane < N_EXPERTS + N_GROUPS, logits, neg), neg)
    mg = jnp.max(lg, axis=-1, keepdims=True)
    p_grp = 1.0 / jnp.sum(jnp.exp(lg - mg), axis=-1, keepdims=True)
    g_idx = _first_argmax(lg, mg, lane) - N_EXPERTS
    in_group = jnp.floor(lane * (1.0 / EXPERTS_PER_GROUP)) == g_idx
    le = jnp.where(in_group, logits, neg)
    m1 = jnp.max(le, axis=-1, keepdims=True)
    i1 = _first_argmax(le, m1, lane)
    le2 = jnp.where(lane == i1, neg, le)
    m2 = jnp.max(le2, axis=-1, keepdims=True)
    i2 = _first_argmax(le2, m2, lane)
    e2 = jnp.exp(m2 - m1)
    w1 = p_grp / (1.0 + e2)
    w2 = w1 * e2
    gate = jnp.where(lane == i1, w1, 0.0) + jnp.where(lane == i2, w2, 0.0)
    route = jnp.where(lane == 0, i1, jnp.where(lane == 1, i2, jnp.where(lane == 2, w1, jnp.where(lane == 3, w2, 0.0))))
    return gate, route


def _split_dot(x_hi, x_lo, w_hi_ref, w_lo_ref):
    w_hi = w_hi_ref[...]
    return _dot(x_hi, w_hi) + (_dot(x_lo, w_hi) + _dot(x_hi, w_lo_ref[...]))


def _oproj_kernel(m_ref, wo_ref, x_ref, gt_ref, g2_ref, sc_ref, sh_ref, wrh_ref, wrl_ref, br_ref,
                  x1_ref, h2_ref, gate_ref, route_ref):
    x1 = x_ref[...] + gt_ref[...] * _dot(m_ref[...], wo_ref[...])
    x1_ref[...] = x1
    h2 = _rms_mod(x1, g2_ref[...], sc_ref[...], sh_ref[...])
    hi = h2.astype(BF16)
    h2_ref[...] = hi
    lo = (h2 - hi.astype(F32)).astype(BF16)
    gate_ref[...], route_ref[...] = _router(_split_dot(hi, lo, wrh_ref, wrl_ref) + br_ref[...])


def _pad_lanes(w):
    return jnp.pad(w, ((0, 0), (0, LANES - w.shape[-1])))


def _hi_lo(w):
    hi = w.astype(BF16)
    return hi, (w - hi.astype(F32)).astype(BF16)


def oproj_router(m, wo, x, gt, g2, sc, sh, w_rg, b_rg, w_re, b_re, tm):
    t, d = x.shape
    wrh, wrl = _hi_lo(_pad_lanes(jnp.concatenate([w_re, w_rg], axis=1)))
    br = _pad_lanes(jnp.concatenate([b_re, b_rg]).reshape(1, -1))
    full = lambda shape: pl.BlockSpec(shape, lambda i: (0, 0))
    return pl.pallas_call(
        _oproj_kernel,
        grid=(t // tm,),
        in_specs=[
            pl.BlockSpec((tm, d), lambda i: (i, 0)),
            full((d, d)),
            pl.BlockSpec((tm, d), lambda i: (i, 0)),
            _row_spec(gt.shape[0], tm, d),
            full((1, d)),
            _row_spec(sc.shape[0], tm, d),
            _row_spec(sh.shape[0], tm, d),
            full((d, LANES)), full((d, LANES)), full((1, LANES)),
        ],
        out_specs=[
            pl.BlockSpec((tm, d), lambda i: (i, 0)),
            pl.BlockSpec((tm, d), lambda i: (i, 0)),
            pl.BlockSpec((tm, LANES), lambda i: (i, 0)),
            pl.BlockSpec((tm, LANES), lambda i: (i, 0)),
        ],
        out_shape=[
            jax.ShapeDtypeStruct((t, d), F32),
            jax.ShapeDtypeStruct((t, d), BF16),
            jax.ShapeDtypeStruct((t, LANES), F32),
            jax.ShapeDtypeStruct((t, LANES), F32),
        ],
        compiler_params=_cparams("parallel"),
        name="oproj_router",
    )(m, wo, x, gt, g2.reshape(1, d), sc, sh, wrh, wrl, br)


def _moe_kernel(h_ref, gate_ref, wg_ref, wu_ref, wd_ref, x_ref, gt_ref, o_ref, wg16_ref, wu16_ref, wd16_ref, acc_ref):
    e = pl.program_id(0)

    @pl.when(e == 0)
    def _():
        acc_ref[...] = jnp.zeros_like(acc_ref)

    wg = wg_ref[0].astype(BF16)
    wu = wu_ref[0].astype(BF16)
    wd = wd_ref[0].astype(BF16)
    wg16_ref[0] = wg
    wu16_ref[0] = wu
    wd16_ref[0] = wd
    h = h_ref[...]
    gate = gate_ref[...]
    lane = lax.broadcasted_iota(jnp.int32, gate.shape, 1)
    ge = jnp.sum(jnp.where(lane == e, gate, 0.0), axis=-1, keepdims=True)
    act = (_silu(_dot(h, wg)) * _dot(h, wu)) * ge
    acc_ref[...] += _dot(act.astype(BF16), wd)

    @pl.when(e == pl.num_programs(0) - 1)
    def _():
        o_ref[...] = x_ref[...] + gt_ref[...] * acc_ref[...]


def moe_dense(h2, gate, w_eg, w_eu, w_ed, layer, x1, gt):
    t, d = x1.shape
    _, n_e, _, f = w_eg.shape
    whole = lambda width: pl.BlockSpec((t, width), lambda e: (0, 0))
    up_in = pl.BlockSpec((None, 1, d, f), lambda e: (layer, e, 0, 0))
    down_in = pl.BlockSpec((None, 1, f, d), lambda e: (layer, e, 0, 0))
    up = pl.BlockSpec((1, d, f), lambda e: (e, 0, 0))
    down = pl.BlockSpec((1, f, d), lambda e: (e, 0, 0))
    return pl.pallas_call(
        _moe_kernel,
        grid=(n_e,),
        in_specs=[whole(d), whole(LANES), up_in, up_in, down_in, whole(d),
                  pl.BlockSpec((gt.shape[0], d), lambda e: (0, 0))],
        out_specs=[whole(d), up, up, down],
        out_shape=[
            jax.ShapeDtypeStruct((t, d), F32),
            jax.ShapeDtypeStruct((n_e, d, f), BF16),
            jax.ShapeDtypeStruct((n_e, d, f), BF16),
            jax.ShapeDtypeStruct((n_e, f, d), BF16),
        ],
        scratch_shapes=[pltpu.VMEM((t, d), F32)],
        compiler_params=_cparams("arbitrary"),
        name="moe_dense",
    )(h2, gate, w_eg, w_eu, w_ed, x1, gt)


SEG_ALIGN = BF16_ROWS
EXPERT_ROWS = 128
STEP_EXPERTS = 4
SORT_ROWS = MXU_DIM


ROUTE_ROWS = 8


def _route_pos_kernel(route_ref, col_ref, row_ref, seg_ref):
    route = route_ref[...]
    tm = route.shape[0]
    lane = lax.broadcasted_iota(jnp.int32, route.shape, 1).astype(F32)
    oh1 = lane == route[:, 0:1]
    oh2 = lane == route[:, 1:2]
    both = jnp.where(oh1, 1.0, jnp.where(oh2, 1.0, 0.0))
    before = (lax.broadcasted_iota(jnp.int32, (tm, tm), 1) < lax.broadcasted_iota(jnp.int32, (tm, tm), 0))
    earlier = _dot(jnp.where(before, 1.0, 0.0).astype(BF16), both.astype(BF16))
    units = jnp.floor((jnp.sum(both, axis=0, keepdims=True) + (SEG_ALIGN - 1)) * (1.0 / SEG_ALIGN))
    lower = (lax.broadcasted_iota(jnp.int32, (LANES, LANES), 0) < lax.broadcasted_iota(jnp.int32, (LANES, LANES), 1))
    start = SEG_ALIGN * _dot(jnp.broadcast_to(units, (ROUTE_ROWS, LANES)).astype(BF16),
                             jnp.where(lower, 1.0, 0.0).astype(BF16))[0:1]
    size = SEG_ALIGN * units
    spot = earlier + start
    pos1 = jnp.sum(jnp.where(oh1, spot, 0.0), axis=-1, keepdims=True)
    pos2 = jnp.sum(jnp.where(oh2, spot, 0.0), axis=-1, keepdims=True)
    col = jnp.where(lane == 0, pos1, jnp.where(lane == 1, pos2, route))
    col_ref[...] = col
    row_ref[0] = jnp.transpose(col)[0:ROUTE_ROWS, :]
    sub = lax.broadcasted_iota(jnp.int32, (ROUTE_ROWS, LANES), 0)
    trips = jnp.floor((size + (EXPERT_ROWS - 1)) * (1.0 / EXPERT_ROWS))
    seg_ref[0] = jnp.where(sub == 0, start, jnp.where(sub == 1, start + size, jnp.where(sub == 2, trips, 0.0)))


def route_pos(route, tm):
    t = route.shape[0]
    nt = t // tm
    col, row, seg = pl.pallas_call(
        _route_pos_kernel,
        grid=(nt,),
        in_specs=[pl.BlockSpec((tm, LANES), lambda i: (i, 0))],
        out_specs=[
            pl.BlockSpec((tm, LANES), lambda i: (i, 0)),
            pl.BlockSpec((1, ROUTE_ROWS, tm), lambda i: (i, 0, 0)),
            pl.BlockSpec((1, ROUTE_ROWS, LANES), lambda i: (i, 0, 0)),
        ],
        out_shape=[
            jax.ShapeDtypeStruct((t, LANES), F32),
            jax.ShapeDtypeStruct((nt, ROUTE_ROWS, tm), F32),
            jax.ShapeDtypeStruct((nt, ROUTE_ROWS, LANES), F32),
        ],
        compiler_params=_cparams("parallel"),
        name="route_pos",
    )(route)
    seg = seg[:, 0:3, 0:N_EXPERTS].astype(jnp.int32)
    return col, row, seg[:, 0].reshape(-1), seg[:, 1].reshape(-1), seg[:, 2].reshape(-1)


def _moe_routed_kernel(start_ref, end_ref, trips_ref, h_ref, col_ref, row_ref, wg_ref, wu_ref, wd_ref, o_ref,
                       xy_ref, wrow_ref):
    i = pl.program_id(0)
    e = pl.program_id(1)
    n_e = pl.num_programs(1)
    tm = h_ref.shape[0]
    rows = xy_ref.shape[0]

    @pl.when(e == 0)
    def _():
        pos1, pos2 = row_ref[0, 0:1, :], row_ref[0, 1:2, :]
        w1, w2 = row_ref[0, 2:3, :], row_ref[0, 3:4, :]

        def sort(c, carry):
            r0 = pl.multiple_of(c * SORT_ROWS, SORT_ROWS)
            row = (r0 + lax.broadcasted_iota(jnp.int32, (SORT_ROWS, tm), 0)).astype(F32)
            hit1 = pos1 == row
            hit2 = pos2 == row
            pick = jnp.where(hit1, 1.0, jnp.where(hit2, 1.0, 0.0)).astype(BF16)
            xy_ref[pl.ds(r0, SORT_ROWS), :] = _dot(pick, h_ref[...]).astype(BF16)
            wrow_ref[pl.ds(r0, SORT_ROWS), :] = jnp.sum(
                jnp.where(hit1, w1, jnp.where(hit2, w2, 0.0)), axis=-1, keepdims=True)
            return carry
        lax.fori_loop(0, rows // SORT_ROWS, sort, 0)

    for k in range(STEP_EXPERTS):
        seg = (i * n_e + e) * STEP_EXPERTS + k
        seg_start = start_ref[seg]
        seg_end = end_ref[seg]

        def expert(c, carry, k=k, seg_start=seg_start, seg_end=seg_end):
            r0 = pl.multiple_of(seg_start + c * EXPERT_ROWS, SEG_ALIGN)
            xs = xy_ref[pl.ds(r0, EXPERT_ROWS), :]
            w_row = wrow_ref[pl.ds(r0, EXPERT_ROWS), :]
            act = (_silu(_dot(xs, wg_ref[k])) * _dot(xs, wu_ref[k])) * w_row
            y = _dot(act.astype(BF16), wd_ref[k]).astype(BF16)
            row = r0 + lax.broadcasted_iota(jnp.int32, (EXPERT_ROWS, 1), 0)
            xy_ref[pl.ds(r0, EXPERT_ROWS), :] = jnp.where(row < seg_end, y, xs)
            return carry

        lax.fori_loop(0, trips_ref[seg], expert, 0)

    @pl.when(e == n_e - 1)
    def _():
        def combine(c, carry):
            t0 = pl.multiple_of(c * SORT_ROWS, SORT_ROWS)
            row = lax.broadcasted_iota(jnp.int32, (SORT_ROWS, rows), 1).astype(F32)
            hit1 = col_ref[pl.ds(t0, SORT_ROWS), 0:1] == row
            hit2 = col_ref[pl.ds(t0, SORT_ROWS), 1:2] == row
            pick = jnp.where(hit1, 1.0, jnp.where(hit2, 1.0, 0.0)).astype(BF16)
            o_ref[pl.ds(t0, SORT_ROWS), :] = _dot(pick, xy_ref[...])
            return carry
        lax.fori_loop(0, tm // SORT_ROWS, combine, 0)


def moe_routed(h2, route, w_eg, w_eu, w_ed, tm):
    t, d = h2.shape
    n_e, _, f = w_eg.shape
    rows = _round_up(2 * tm + n_e * (SEG_ALIGN - 1) + EXPERT_ROWS, SORT_ROWS)
    col, row, start, end, trips = route_pos(route, tm)
    grid_spec = pltpu.PrefetchScalarGridSpec(
        num_scalar_prefetch=3,
        grid=(t // tm, n_e // STEP_EXPERTS),
        in_specs=[
            pl.BlockSpec((tm, d), lambda i, e, *_: (i, 0)),
            pl.BlockSpec((tm, LANES), lambda i, e, *_: (i, 0)),
            pl.BlockSpec((1, ROUTE_ROWS, tm), lambda i, e, *_: (i, 0, 0)),
            pl.BlockSpec((STEP_EXPERTS, d, f), lambda i, e, *_: (e, 0, 0)),
            pl.BlockSpec((STEP_EXPERTS, d, f), lambda i, e, *_: (e, 0, 0)),
            pl.BlockSpec((STEP_EXPERTS, f, d), lambda i, e, *_: (e, 0, 0)),
        ],
        out_specs=pl.BlockSpec((tm, d), lambda i, e, *_: (i, 0), pipeline_mode=pl.Buffered(1)),
        scratch_shapes=[pltpu.VMEM((rows, d), BF16), pltpu.VMEM((rows, 1), F32)],
    )
    return pl.pallas_call(
        _moe_routed_kernel,
        grid_spec=grid_spec,
        out_shape=jax.ShapeDtypeStruct((t, d), F32),
        compiler_params=_cparams("parallel", "arbitrary", vmem=VMEM_LIMIT_MOE),
        name="moe_routed",
    )(start, end, trips, h2, col, row, w_eg, w_eu, w_ed)


def _mix_rows(x, h, mods, wts, layer, tm, k_all, v_all, conv_fn, attn_fn):
    _, _, gt1, sh2, sc2, _ = mods
    d = x.shape[1]
    w_in = wts["w_in"]
    c_ch = wts["w_co"].shape[0]
    attn_w = wts["w_ao"].shape[0]
    col_q = 2 * c_ch
    col_g = col_q + 3 * attn_w
    u = glu_proj(h, w_in, layer, 0, c_ch, c_ch, tm)
    q, k_all, k16, v_all, v16 = qkv_proj(h, w_in, layer, col_q, attn_w, k_all, v_all, tm)
    act, new_conv = conv_fn(u)
    att = attn_fn(q, k16, v16)
    m = merge(h, act, att, w_in, layer, col_g, col_g + d, wts["w_co"], wts["w_ao"], tm)
    x1, h2, gate, route = oproj_router(m, wts["w_o"], x, gt1, wts["g_n2"], sc2, sh2,
                                       wts["w_rg"], wts["b_rg"], wts["w_re"], wts["b_re"], min(tm, 256))
    return x1, h2, gate, route, k_all, v_all, new_conv


def kernel(x_prompt, x_sample, cache_k, cache_v, state_conv, page_table, c_prompt, c_sample, w_ada, b_ada, g_n1, w_in, b_sb, w_dw, b_dw, ln_g, ln_b, w_conv_out, w_attn_out, w_o, g_n2, w_rg, b_rg, w_re, b_re, w_eg, w_eu, w_ed, g_final):
    batch, seq, d = x_prompt.shape
    n_seq, t_new, _ = x_sample.shape
    assert batch == 1 and t_new == 1, "one prompt sequence and one new token per sample sequence"
    n_layers = w_in.shape[0]
    c_ch = w_dw.shape[-1]
    attn_w = N_HEADS * HEAD_DIM
    hist = CONV_KERNEL - 1
    assert seq >= hist

    mod_rows = n_seq + BF16_ROWS
    c_all = jnp.zeros((mod_rows, d), F32).at[:n_seq].set(c_sample).at[n_seq:n_seq + 1].set(c_prompt)
    mod = ada_mod(c_all, w_ada, b_ada)

    tm_p = min(1024, seq)
    tm_n = min(512, seq)
    xp = x_prompt.reshape(seq, d)
    xs = x_sample.reshape(n_seq, d)
    pending = None
    zero_halo = jnp.zeros((HALO, c_ch), F32)
    zeros_row = jnp.zeros((1, d), F32)
    kp = jnp.zeros((n_layers, seq, attn_w), F32)
    vp = jnp.zeros((n_layers, seq, attn_w), F32)
    ks = jnp.zeros((n_layers, n_seq, attn_w), F32)
    vs = jnp.zeros((n_layers, n_seq, attn_w), F32)
    cs = jnp.zeros(state_conv.shape, F32)
    cp = []
    w_in16 = w_in.astype(BF16)
    for l in range(n_layers):
        wts = dict(
            g_n2=g_n2[l], w_in=w_in16,
            w_co=w_conv_out[l].astype(BF16), w_ao=w_attn_out[l].astype(BF16), w_o=w_o[l].astype(BF16),
            w_rg=w_rg[l], b_rg=b_rg[l], w_re=w_re[l], b_re=b_re[l],
        )
        mods_s = tuple(mod[l, :n_seq, i * d:(i + 1) * d] for i in range(6))
        mods_p = tuple(mod[l, n_seq:n_seq + 1, i * d:(i + 1) * d] for i in range(6))

        def conv_p(u, l=l):
            act = conv_prompt(u, zero_halo, w_dw[l], b_dw[l], ln_g[l], ln_b[l])
            return act, u[seq - hist:]

        def attn_p(q, k16, v16, l=l):
            return attn_prompt(q, k16, v16, b_sb[l])

        def conv_s(u, l=l, cs=cs):
            return conv_sample(u, state_conv, l, w_dw[l], b_dw[l], ln_g[l], ln_b[l], cs)

        def attn_s(q, k16, v16, l=l):
            att = attn_sample(q.reshape(n_seq, N_HEADS, HEAD_DIM), cache_k, cache_v, l, page_table, b_sb[l])
            return att.reshape(n_seq, attn_w)

        hs = norm_mod(xs, g_n1[l], mods_s[1], mods_s[0], BF16, n_seq)
        x1s, h2s, gate_s, _, ks, vs, cs = _mix_rows(xs, hs, mods_s, wts, l, n_seq, ks, vs, conv_s, attn_s)
        xs, weg, weu, wed = moe_dense(h2s, gate_s, w_eg, w_eu, w_ed, l, x1s, mods_s[5])

        if pending is None:
            hp = norm_mod(xp, g_n1[l], mods_p[1], mods_p[0], BF16, tm_p)
        else:
            xp, hp = resid_norm(*pending, g_n1[l], mods_p[1], mods_p[0], BF16, tm_n)
        x1, h2, _, route, kp, vp, cp_l = _mix_rows(xp, hp, mods_p, wts, l, tm_p, kp, vp, conv_p, attn_p)
        pending = (x1, moe_routed(h2, route, weg, weu, wed, tm_p), mods_p[5])
        cp.append(cp_l.reshape(1, hist, c_ch))

    _, y_prompt = resid_norm(*pending, g_final, zeros_row, zeros_row, F32, tm_n)
    y_sample = norm_mod(xs, g_final, zeros_row, zeros_row, F32, n_seq)
    return (y_prompt.reshape(1, seq, d), y_sample.reshape(n_seq, 1, d),
            kp.reshape(n_layers, 1, seq, N_HEADS, HEAD_DIM), vp.reshape(n_layers, 1, seq, N_HEADS, HEAD_DIM),
            jnp.stack(cp),
            ks.reshape(n_layers, n_seq, 1, N_HEADS, HEAD_DIM), vs.reshape(n_layers, n_seq, 1, N_HEADS, HEAD_DIM), cs)
```
